```python
import math
import jax, jax.numpy as jnp
from jax import lax
import numpy as np

D_MODEL = 2048
BATCH = 2
SEQ = 16384
DEPTH = 4

SB_HEADS = 4
SB_HEAD_DIM = 128
SB_WIDTH = SB_HEADS * SB_HEAD_DIM
Q_BLOCK = 128
SSD_HEADS = 24
SSD_HEAD_DIM = 64
SSD_WIDTH = SSD_HEADS * SSD_HEAD_DIM
SSD_GROUPS = 4
SSD_STATE = 128
SSD_CONV = 4
SSD_CHUNK = 128
SSD_CONV_CH = SSD_WIDTH + 2 * SSD_GROUPS * SSD_STATE
AB_IN = 3 * SB_WIDTH + SSD_WIDTH + SSD_CONV_CH + SSD_HEADS
AB_SPLITS = [SB_WIDTH, 2 * SB_WIDTH, 3 * SB_WIDTH, 3 * SB_WIDTH + SSD_WIDTH, 3 * SB_WIDTH + SSD_WIDTH + SSD_CONV_CH]
AB_OUT = SB_WIDTH + SSD_WIDTH
LRU_WIDTH = 1536
LRU_BLOCK = 128
LRU_BLOCKS = LRU_WIDTH // LRU_BLOCK
LRU_CONV = 4
LRU_C = 8.0
S5_GROUPS = 32
S5_GROUP_CH = 16
S5_WIDTH = S5_GROUPS * S5_GROUP_CH
S5_STATE = 64
S5_CHUNK = 128
CD_IN = 2 * LRU_WIDTH + S5_WIDTH
CD_OUT = LRU_WIDTH + S5_WIDTH
D_FF = 4096
FFN_CONV = 3
EPS = 1e-6
N_EVEN = (DEPTH + 1) // 2
N_ODD = DEPTH // 2

kernel_name = 'hybrid_sb_ssd_rglru_s5_trunk'


def rmsnorm(x, g):
    xf = x.astype(jnp.float32)
    xf = xf * lax.rsqrt(jnp.mean(jnp.square(xf), axis=-1, keepdims=True) + EPS)
    return (xf * g.astype(jnp.float32)).astype(x.dtype)


def causal_dwconv(x, w, b):
    k, ch = w.shape
    y = lax.conv_general_dilated(x, w[:, None, :].astype(x.dtype), window_strides=(1,),
                                 padding=[(k - 1, 0)], dimension_numbers=('NWC', 'WIO', 'NWC'),
                                 feature_group_count=ch)
    return y + b.astype(y.dtype)


def _sb_schedule(nblk):
    half = (nblk + 1) // 2
    steps = nblk + 1
    ra = np.arange(half)[:, None]
    rb = nblk - 1 - ra
    k = np.arange(steps)[None, :]
    second = k > ra
    rows = np.where(second, rb, ra)
    kbs = np.where(second, rb - (k - ra - 1), ra - k)
    switch = k == ra + 1
    first = (k == 0) | switch
    return (jnp.asarray(ra[:, 0], jnp.int32), jnp.asarray(rb[:, 0], jnp.int32),
            jnp.asarray(rows, jnp.int32), jnp.asarray(kbs, jnp.int32),
            jnp.asarray(first), jnp.asarray(switch))


def stick_breaking_attention(q, k, v):
    f32 = jnp.float32
    bsz, nh, seq, dh = q.shape
    nblk = seq // Q_BLOCK
    qf = q.astype(f32) * (dh ** -0.5)
    kf = k.astype(f32)
    vf = v.astype(f32)
    ra, rb, rows, kbs, first, switch = _sb_schedule(nblk)
    tri = jnp.tril(jnp.ones((Q_BLOCK, Q_BLOCK), f32))
    loc = jnp.arange(Q_BLOCK)
    strict = loc[None, :] < loc[:, None]
    zeros_l = jnp.zeros((bsz, nh, Q_BLOCK), f32)
    zeros_a = jnp.zeros((bsz, nh, Q_BLOCK, dh), f32)

    def pair(args):
        rows_p, kbs_p, first_p, switch_p = args

        def step(carry, inp):
            logsum, acc, saved = carry
            row, kb, fst, sw = inp
            saved = jnp.where(sw, acc, saved)
            logsum = jnp.where(fst, 0.0, logsum)
            acc = jnp.where(fst, 0.0, acc)
            q_blk = lax.dynamic_slice_in_dim(qf, row * Q_BLOCK, Q_BLOCK, axis=2)
            k_blk = lax.dynamic_slice_in_dim(kf, kb * Q_BLOCK, Q_BLOCK, axis=2)
            v_blk = lax.dynamic_slice_in_dim(vf, kb * Q_BLOCK, Q_BLOCK, axis=2)
            z = jnp.einsum('bhqd,bhkd->bhqk', q_blk, k_blk)
            mask = (kb < row) | strict
            log_1mb = jnp.where(mask, -jax.nn.softplus(z), 0.0)
            suffix = log_1mb @ tri
            w = jnp.exp(jnp.where(mask, z + suffix + logsum[..., None], -jnp.inf))
            acc = acc + jnp.einsum('bhqk,bhkd->bhqd', w, v_blk)
            logsum = logsum + suffix[..., 0]
            return (logsum, acc, saved), None

        (_, acc, saved), _ = lax.scan(step, (zeros_l, zeros_a, zeros_a), (rows_p, kbs_p, first_p, switch_p))
        return saved, acc

    out_a, out_b = lax.map(pair, (rows, kbs, first, switch))
    out = jnp.zeros((nblk, bsz, nh, Q_BLOCK, dh), f32).at[ra].set(out_a).at[rb].set(out_b)
    return out.transpose(1, 0, 3, 2, 4).reshape(bsz, seq, nh * dh)


def ssd_mixer(z, xbc, dt_raw, conv_w, conv_b, dt_bias, a_log, d_skip, norm_g):
    f32 = jnp.float32
    bsz, seq, _ = xbc.shape
    nc, L = seq // SSD_CHUNK, SSD_CHUNK
    G, R, P, N = SSD_GROUPS, SSD_HEADS // SSD_GROUPS, SSD_HEAD_DIM, SSD_STATE
    xbc = jax.nn.silu(causal_dwconv(xbc, conv_w, conv_b).astype(f32))
    xs, bm, cm = jnp.split(xbc, [SSD_WIDTH, SSD_WIDTH + G * N], axis=-1)
    xs = xs.reshape(bsz, nc, L, G, R, P)
    bm = bm.reshape(bsz, nc, L, G, N)
    cm = cm.reshape(bsz, nc, L, G, N)
    dt = jax.nn.softplus(dt_raw.astype(f32) + dt_bias.astype(f32)).reshape(bsz, nc, L, G, R)
    a = -jnp.exp(a_log.astype(f32)).reshape(G, R)
    cs = jnp.cumsum(dt * a, axis=2)
    xdt = xs * dt[..., None]
    causal = jnp.tril(jnp.ones((L, L), dtype=bool))[:, :, None, None]
    decay = jnp.exp(jnp.where(causal, cs[:, :, :, None] - cs[:, :, None, :], -jnp.inf))
    cb = jnp.einsum('bclgn,bcsgn->bclsg', cm, bm)
    y_diag = jnp.einsum('bclsgr,bcsgrp->bclgrp', cb[..., None] * decay, xdt)
    to_end = jnp.exp(cs[:, :, -1:] - cs)
    chunk_states = jnp.einsum('bclgn,bclgrp->bcgrpn', bm, xdt * to_end[..., None])
    chunk_decay = jnp.exp(cs[:, :, -1])

    def carry_state(h, inp):
        st, dec = inp
        return h * dec[..., None, None] + st, h

    h0 = jnp.zeros((bsz, G, R, P, N), f32)
    _, prev = lax.scan(carry_state, h0, (jnp.moveaxis(chunk_states, 1, 0), jnp.moveaxis(chunk_decay, 1, 0)))
    prev = jnp.moveaxis(prev, 0, 1)
    y_off = jnp.einsum('bclgn,bcgrpn->bclgrp', cm, prev) * jnp.exp(cs)[..., None]
    y = y_diag + y_off + xs * d_skip.astype(f32).reshape(G, R)[:, :, None]
    y = y.reshape(bsz, seq, SSD_WIDTH) * jax.nn.silu(z.astype(f32))
    return rmsnorm(y, norm_g)


def rglru_mixer(gate_in, x_in, conv_w, conv_b, w_a, b_a, w_i, b_i, lam):
    f32 = jnp.float32
    bsz, seq, W = x_in.shape
    gate = jax.nn.gelu(gate_in.astype(f32))
    xc = causal_dwconv(x_in, conv_w, conv_b).astype(f32)
    xb = xc.reshape(bsz, seq, LRU_BLOCKS, LRU_BLOCK)
    r = jax.nn.sigmoid(jnp.einsum('bshi,hij->bshj', xb, w_a.astype(f32)).reshape(bsz, seq, W) + b_a.astype(f32))
    gate_i = jax.nn.sigmoid(jnp.einsum('bshi,hij->bshj', xb, w_i.astype(f32)).reshape(bsz, seq, W) + b_i.astype(f32))
    log_a = -LRU_C * r * jax.nn.softplus(-lam.astype(f32))
    a = jnp.exp(log_a)
    u = jnp.sqrt(-jnp.expm1(2.0 * log_a)) * (gate_i * xc)

    def step(h, inp):
        a_t, u_t = inp
        h = a_t * h + u_t
        return h, h

    _, hs = lax.scan(step, jnp.zeros((bsz, W), f32), (jnp.moveaxis(a, 1, 0), jnp.moveaxis(u, 1, 0)))
    return jnp.moveaxis(hs, 0, 1) * gate


def _complex_combine(e1, e2):
    a1r, a1i, b1r, b1i = e1
    a2r, a2i, b2r, b2i = e2
    return (a2r * a1r - a2i * a1i, a2r * a1i + a2i * a1r,
            a2r * b1r - a2i * b1i + b2r, a2r * b1i + a2i * b1r + b2i)


def s5_mixer(u, log_dt, a_re, a_im, b_re, b_im, c_re, c_im, d_skip, w_glu, b_glu):
    f32 = jnp.float32
    bsz, seq, _ = u.shape
    G, GC, N, L = S5_GROUPS, S5_GROUP_CH, S5_STATE, S5_CHUNK
    nc = seq // L
    uf = u.astype(f32).reshape(bsz, seq, G, GC)
    dt = jnp.exp(log_dt.astype(f32))[:, None]
    lr = jnp.minimum(a_re.astype(f32), -1e-4)
    li = a_im.astype(f32)
    mag = jnp.exp(dt * lr)
    ab_r = mag * jnp.cos(dt * li)
    ab_i = mag * jnp.sin(dt * li)
    am1_r = jnp.expm1(dt * lr) * jnp.cos(dt * li) - 2.0 * jnp.square(jnp.sin(0.5 * dt * li))
    den = lr * lr + li * li
    f_r = (am1_r * lr + ab_i * li) / den
    f_i = (ab_i * lr - am1_r * li) / den
    br, bi = b_re.astype(f32), b_im.astype(f32)
    bb_r = f_r[..., None] * br - f_i[..., None] * bi
    bb_i = f_r[..., None] * bi + f_i[..., None] * br
    bu_r = jnp.einsum('bsgc,gnc->bsgn', uf, bb_r).reshape(bsz, nc, L, G, N)
    bu_i = jnp.einsum('bsgc,gnc->bsgn', uf, bb_i).reshape(bsz, nc, L, G, N)
    a_r_t = jnp.broadcast_to(ab_r, (1, 1, L, G, N))
    a_i_t = jnp.broadcast_to(ab_i, (1, 1, L, G, N))
    _, _, loc_r, loc_i = lax.associative_scan(_complex_combine, (a_r_t, a_i_t, bu_r, bu_i), axis=2)
    powers = jnp.arange(1, L + 1, dtype=f32)[:, None, None] * dt[None]
    pw_mag = jnp.exp(powers * lr)
    pw_r = pw_mag * jnp.cos(powers * li)
    pw_i = pw_mag * jnp.sin(powers * li)

    def carry_state(h, end):
        hr, hi = h
        er, ei = end
        nr = pw_r[-1] * hr - pw_i[-1] * hi + er
        ni = pw_r[-1] * hi + pw_i[-1] * hr + ei
        return (nr, ni), (hr, hi)

    h0 = jnp.zeros((bsz, G, N), f32)
    _, (p_r, p_i) = lax.scan(carry_state, (h0, h0),
                             (jnp.moveaxis(loc_r[:, :, -1], 1, 0), jnp.moveaxis(loc_i[:, :, -1], 1, 0)))
    p_r = jnp.moveaxis(p_r, 0, 1)[:, :, None]
    p_i = jnp.moveaxis(p_i, 0, 1)[:, :, None]
    s_r = (loc_r + pw_r * p_r - pw_i * p_i).reshape(bsz, seq, G, N)
    s_i = (loc_i + pw_r * p_i + pw_i * p_r).reshape(bsz, seq, G, N)
    y = (jnp.einsum('bsgn,gcn->bsgc', s_r, c_re.astype(f32))
         - jnp.einsum('bsgn,gcn->bsgc', s_i, c_im.astype(f32))
         + d_skip.astype(f32).reshape(G, GC) * uf)
    y = jax.nn.gelu(y.reshape(bsz, seq, S5_WIDTH))
    return y * jax.nn.sigmoid(y @ w_glu.astype(f32) + b_glu.astype(f32))


def attn_ssd_layer(y, w_in, conv_w, conv_b, dt_bias, a_log, d_skip, ssd_norm, w_out):
    bsz, seq, _ = y.shape
    q, k, v, z, xbc, dt_raw = jnp.split(y @ w_in, AB_SPLITS, axis=-1)

    def heads(t):
        return t.reshape(bsz, seq, SB_HEADS, SB_HEAD_DIM).transpose(0, 2, 1, 3)

    o_a = stick_breaking_attention(heads(q), heads(k), heads(v))
    o_b = ssd_mixer(z, xbc, dt_raw, conv_w, conv_b, dt_bias, a_log, d_skip, ssd_norm)
    return jnp.concatenate([o_a, o_b], axis=-1).astype(y.dtype) @ w_out


def lru_s5_layer(y, w_in, conv_w, conv_b, w_a, b_a, w_i, b_i, lam, log_dt, a_re, a_im,
                 b_re, b_im, c_re, c_im, d_skip, w_glu, b_glu, w_out):
    gate_in, x_in, u = jnp.split(y @ w_in, [LRU_WIDTH, 2 * LRU_WIDTH], axis=-1)
    o_c = rglru_mixer(gate_in, x_in, conv_w, conv_b, w_a, b_a, w_i, b_i, lam)
    o_d = s5_mixer(u, log_dt, a_re, a_im, b_re, b_im, c_re, c_im, d_skip, w_glu, b_glu)
    return jnp.concatenate([o_c, o_d], axis=-1).astype(y.dtype) @ w_out


def conv_ffn(y, w_up, conv_w, conv_b, w_down):
    val, gate = jnp.split(y @ w_up, 2, axis=-1)
    gate = causal_dwconv(gate, conv_w, conv_b)
    return (jax.nn.silu(gate) * val) @ w_down


def setup_inputs(seed: int = 0) -> dict:
    key = jax.random.key(seed)
    keys = iter(jax.random.split(key, 64))

    def nrm(shape, scale):
        return jax.random.normal(next(keys), shape, jnp.float32) * scale

    def uni(shape, lo, hi):
        return jax.random.uniform(next(keys), shape, jnp.float32, lo, hi)

    E, O, D = N_EVEN, N_ODD, D_MODEL
    ssd_dt = jnp.exp(uni((E, SSD_HEADS), math.log(1e-3), math.log(1e-1)))
    lru_a = uni((O, LRU_WIDTH), 0.9, 0.999)
    s5_freq = math.pi * jnp.arange(S5_STATE, dtype=jnp.float32)
    return {
        'x': nrm((BATCH, SEQ, D), 1.0),
        'norm_mix': 1.0 + nrm((DEPTH, D), 0.02),
        'norm_ffn': 1.0 + nrm((DEPTH, D), 0.02),
        'norm_final': 1.0 + nrm((D,), 0.02),
        'ab_w_in': nrm((E, D, AB_IN), D ** -0.5),
        'ab_conv_w': nrm((E, SSD_CONV, SSD_CONV_CH), SSD_CONV ** -0.5),
        'ab_conv_b': nrm((E, SSD_CONV_CH), 0.02),
        'ab_dt_bias': ssd_dt + jnp.log(-jnp.expm1(-ssd_dt)),
        'ab_a_log': jnp.log(uni((E, SSD_HEADS), 1.0, 16.0)),
        'ab_d_skip': 1.0 + nrm((E, SSD_HEADS), 0.1),
        'ab_ssd_norm': 1.0 + nrm((E, SSD_WIDTH), 0.02),
        'ab_w_out': nrm((E, AB_OUT, D), AB_OUT ** -0.5),
        'cd_w_in': nrm((O, D, CD_IN), D ** -0.5),
        'cd_conv_w': nrm((O, LRU_CONV, LRU_WIDTH), LRU_CONV ** -0.5),
        'cd_conv_b': nrm((O, LRU_WIDTH), 0.02),
        'cd_w_a': nrm((O, LRU_BLOCKS, LRU_BLOCK, LRU_BLOCK), LRU_BLOCK ** -0.5),
        'cd_b_a': nrm((O, LRU_WIDTH), 0.1),
        'cd_w_i': nrm((O, LRU_BLOCKS, LRU_BLOCK, LRU_BLOCK), LRU_BLOCK ** -0.5),
        'cd_b_i': nrm((O, LRU_WIDTH), 0.1),
        'cd_lambda': jnp.log(lru_a) - jnp.log1p(-lru_a),
        'cd_log_dt': uni((O, S5_GROUPS), math.log(1e-3), math.log(1e-1)),
        'cd_a_re': -0.5 + nrm((O, S5_GROUPS, S5_STATE), 0.01),
        'cd_a_im': s5_freq + nrm((O, S5_GROUPS, S5_STATE), 0.01),
        'cd_b_re': nrm((O, S5_GROUPS, S5_STATE, S5_GROUP_CH), (2 * S5_GROUP_CH) ** -0.5),
        'cd_b_im': nrm((O, S5_GROUPS, S5_STATE, S5_GROUP_CH), (2 * S5_GROUP_CH) ** -0.5),
        'cd_c_re': nrm((O, S5_GROUPS, S5_GROUP_CH, S5_STATE), S5_STATE ** -0.5),
        'cd_c_im': nrm((O, S5_GROUPS, S5_GROUP_CH, S5_STATE), S5_STATE ** -0.5),
        'cd_d_skip': nrm((O, S5_WIDTH), 1.0),
        'cd_w_glu': nrm((O, S5_WIDTH, S5_WIDTH), S5_WIDTH ** -0.5),
        'cd_b_glu': nrm((O, S5_WIDTH), 0.02),
        'cd_w_out': nrm((O, CD_OUT, D), CD_OUT ** -0.5),
        'ffn_w_up': nrm((DEPTH, D, 2 * D_FF), D ** -0.5),
        'ffn_conv_w': nrm((DEPTH, FFN_CONV, D_FF), FFN_CONV ** -0.5),
        'ffn_conv_b': nrm((DEPTH, D_FF), 0.02),
        'ffn_w_down': nrm((DEPTH, D_FF, D), D_FF ** -0.5),
    }


def reference(x, norm_mix, norm_ffn, norm_final, ab_w_in, ab_conv_w, ab_conv_b, ab_dt_bias, ab_a_log,
              ab_d_skip, ab_ssd_norm, ab_w_out, cd_w_in, cd_conv_w, cd_conv_b, cd_w_a, cd_b_a, cd_w_i,
              cd_b_i, cd_lambda, cd_log_dt, cd_a_re, cd_a_im, cd_b_re, cd_b_im, cd_c_re, cd_c_im,
              cd_d_skip, cd_w_glu, cd_b_glu, cd_w_out, ffn_w_up, ffn_conv_w, ffn_conv_b, ffn_w_down):
    h = x
    for layer in range(DEPTH):
        y = rmsnorm(h, norm_mix[layer])
        j = layer // 2
        if layer % 2 == 0:
            mix = attn_ssd_layer(y, ab_w_in[j], ab_conv_w[j], ab_conv_b[j], ab_dt_bias[j], ab_a_log[j],
                                 ab_d_skip[j], ab_ssd_norm[j], ab_w_out[j])
        else:
            mix = lru_s5_layer(y, cd_w_in[j], cd_conv_w[j], cd_conv_b[j], cd_w_a[j], cd_b_a[j], cd_w_i[j],
                               cd_b_i[j], cd_lambda[j], cd_log_dt[j], cd_a_re[j], cd_a_im[j], cd_b_re[j],
                               cd_b_im[j], cd_c_re[j], cd_c_im[j], cd_d_skip[j], cd_w_glu[j], cd_b_glu[j],
                               cd_w_out[j])
        h = h + mix
        h = h + conv_ffn(rmsnorm(h, norm_ffn[layer]), ffn_w_up[layer], ffn_conv_w[layer],
                         ffn_conv_b[layer], ffn_w_down[layer])
    return rmsnorm(h, norm_final)
```

```python
import functools

import jax
import jax.numpy as jnp
from jax import lax
from jax.experimental import pallas as pl
from jax.experimental.pallas import tpu as pltpu

F32 = jnp.float32
BF16 = jnp.bfloat16

D_MODEL = 2048
SB_HEADS = 4
SB_HEAD_DIM = 128
SB_WIDTH = SB_HEADS * SB_HEAD_DIM
SSD_HEADS = 24
SSD_HEAD_DIM = 64
SSD_WIDTH = SSD_HEADS * SSD_HEAD_DIM
SSD_GROUPS = 4
SSD_HEADS_PER_GROUP = SSD_HEADS // SSD_GROUPS
SSD_STATE = 128
SSD_CONV = 4
SSD_CHUNK = 128
SSD_BC = SSD_GROUPS * SSD_STATE
AB_MAIN = 3 * SB_WIDTH + 2 * SSD_WIDTH + 2 * SSD_BC
LRU_WIDTH = 1536
LRU_BLOCK = 128
LRU_BLOCKS = LRU_WIDTH // LRU_BLOCK
LRU_CONV = 4
LRU_C = 8.0
S5_GROUPS = 32
S5_GROUP_CH = 16
S5_WIDTH = S5_GROUPS * S5_GROUP_CH
S5_STATE = 64
S5_CHUNK = 128
S5_STATE_WIDTH = S5_GROUPS * S5_STATE
CD_IN = 2 * LRU_WIDTH + S5_WIDTH
D_FF = 4096
FFN_CONV = 3
EPS = 1e-6

LANES = 128
SUBLANES = 8
VMEM_CAP_BYTES = 60000 * 1024

PROJ_TM = 1024
PROJ_TN = 512
OUT_TM = 1024
OUT_TN = 1024
FFN_TM = 512
FFN_TF = 512
SB_TQ = 256
LRU_TT = 256
NORM_TM = 1024
HALO = SUBLANES


def _vmem_limit(est_bytes):
    return int(min(max(est_bytes * 5 // 4, 32 * 1024 * 1024), VMEM_CAP_BYTES))


def _params(semantics, est_bytes):
    return pltpu.CompilerParams(dimension_semantics=semantics, vmem_limit_bytes=_vmem_limit(est_bytes))


def _dot(a, b):
    return jnp.dot(a, b, preferred_element_type=F32)


def _dot_nt(a, b):
    return lax.dot_general(a, b, (((1,), (1,)), ((), ())), preferred_element_type=F32)


def _dot_tn(a, b):
    return lax.dot_general(a, b, (((0,), (0,)), ((), ())), preferred_element_type=F32)


def _split_dot(x, m, terms, x_on_left=True):
    acc = None
    r = x
    for _ in range(terms):
        p = r.astype(BF16)
        d = _dot(p, m) if x_on_left else _dot(m, p)
        acc = d if acc is None else acc + d
        r = r - p.astype(F32)
    return acc


def _softplus(x):
    return jnp.maximum(x, 0.0) + jnp.log(1.0 + jnp.exp(-jnp.abs(x)))


def _sigmoid(x):
    return 1.0 / (1.0 + jnp.exp(-x))


def _silu(x):
    return x / (1.0 + jnp.exp(-x))


def _gelu_tanh(x):
    return 0.5 * x * (1.0 + jnp.tanh(0.7978845608028654 * (x + 0.044715 * (x * x * x))))


def _rms_scale(x):
    return x * lax.rsqrt(jnp.mean(x * x, axis=-1, keepdims=True) + EPS)


def _shift_rows(x, d, fill):
    row = lax.broadcasted_iota(jnp.int32, x.shape, 0)
    return jnp.where(row >= d, pltpu.roll(x, d, axis=0), fill)


def _causal_conv(cur, tail_ref, w, bias, taps):
    rows = cur.shape[0]
    ext = jnp.concatenate([tail_ref[...], cur], axis=0)
    tail_ref[...] = cur[rows - HALO:, :]
    y = bias + w[taps - 1:taps, :] * cur
    for k in range(taps - 1):
        off = HALO - (taps - 1) + k
        y = y + w[k:k + 1, :] * ext[off:off + rows, :]
    return y


def _norm_proj_body(*refs, has_aux):
    if has_aux:
        h_ref, g_ref, w_ref, wa_ref, o_ref, oa_ref, yn_ref = refs
    else:
        h_ref, g_ref, w_ref, o_ref, yn_ref = refs

    @pl.when(pl.program_id(1) == 0)
    def _():
        yn_ref[...] = (_rms_scale(h_ref[...]) * g_ref[...]).astype(BF16)
        if has_aux:
            oa_ref[...] = _dot(yn_ref[...], wa_ref[...])

    o_ref[...] = _dot(yn_ref[...], w_ref[...]).astype(o_ref.dtype)


def _norm_proj(h, g, w, w_aux=None):
    t, d = h.shape
    n = w.shape[1]
    has_aux = w_aux is not None
    in_specs = [pl.BlockSpec((PROJ_TM, d), lambda i, j: (i, 0)),
                pl.BlockSpec((1, d), lambda i, j: (0, 0)),
                pl.BlockSpec((d, PROJ_TN), lambda i, j: (0, j))]
    out_shape = [jax.ShapeDtypeStruct((t, n), BF16)]
    out_specs = [pl.BlockSpec((PROJ_TM, PROJ_TN), lambda i, j: (i, j))]
    args = [h, g.reshape(1, d), w]
    if has_aux:
        in_specs.append(pl.BlockSpec((d, LANES), lambda i, j: (0, 0)))
        out_shape.append(jax.ShapeDtypeStruct((t, LANES), F32))
        out_specs.append(pl.BlockSpec((PROJ_TM, LANES), lambda i, j: (i, 0)))
        args.append(w_aux)
    est = 2 * PROJ_TM * d * 4 + PROJ_TM * d * 2 + 4 * d * PROJ_TN + 4 * PROJ_TM * PROJ_TN + 4 * PROJ_TM * d
    out = pl.pallas_call(
        functools.partial(_norm_proj_body, has_aux=has_aux),
        grid=(t // PROJ_TM, n // PROJ_TN),
        in_specs=in_specs, out_specs=out_specs, out_shape=out_shape,
        scratch_shapes=[pltpu.VMEM((PROJ_TM, d), BF16)],
        compiler_params=_params(("arbitrary", "arbitrary"), est),
        name="norm_proj_aux" if has_aux else "norm_proj",
    )(*args)
    return out if has_aux else out[0]


def _sb_body(q_ref, k_ref, v_ref, o_ref):
    i = pl.program_id(2)
    tq = SB_TQ
    q = q_ref[...]
    row = lax.broadcasted_iota(jnp.int32, (tq, tq), 0)
    col = lax.broadcasted_iota(jnp.int32, (tq, tq), 1)
    tri = (row >= col).astype(BF16)
    strict = col < row
    scale = SB_HEAD_DIM ** -0.5

    def block(kb, masked, logsum, acc):
        start = pl.multiple_of(kb * tq, tq)
        k = k_ref[pl.ds(start, tq), :]
        v = v_ref[pl.ds(start, tq), :]
        z = _dot_nt(q, k) * scale
        log_1mb = -_softplus(z)
        if masked:
            log_1mb = jnp.where(strict, log_1mb, 0.0)
        suffix = _dot(log_1mb.astype(BF16), tri)
        w = jnp.exp(z + suffix + logsum)
        if masked:
            w = jnp.where(strict, w, 0.0)
        acc = acc + _dot(w.astype(BF16), v)
        logsum = logsum + jnp.sum(log_1mb, axis=-1, keepdims=True)
        return logsum, acc

    carry = block(i, True, jnp.zeros((tq, 1), F32), jnp.zeros((tq, SB_HEAD_DIM), F32))
    carry = lax.fori_loop(0, i, lambda n, c: block(i - 1 - n, False, *c), carry)
    o_ref[...] = carry[1].astype(o_ref.dtype)


def _sb_attention(proj, bsz, seq):
    nq = seq // SB_TQ
    est = 4 * seq * SB_HEAD_DIM * 2 + 16 * SB_TQ * SB_TQ * 4
    return pl.pallas_call(
        _sb_body,
        grid=(bsz, SB_HEADS, nq),
        in_specs=[pl.BlockSpec((SB_TQ, SB_HEAD_DIM), lambda b, h, i: (b * nq + i, h)),
                  pl.BlockSpec((seq, SB_HEAD_DIM), lambda b, h, i: (b, SB_HEADS + h)),
                  pl.BlockSpec((seq, SB_HEAD_DIM), lambda b, h, i: (b, 2 * SB_HEADS + h))],
        out_specs=pl.BlockSpec((SB_TQ, SB_HEAD_DIM), lambda b, h, i: (b * nq + i, h)),
        out_shape=jax.ShapeDtypeStruct((bsz * seq, SB_WIDTH), BF16),
        compiler_params=_params(("arbitrary", "arbitrary", "arbitrary"), est),
        name="sb_attention",
    )(proj, proj, proj)


def _ssd_body(z_ref, x_ref, b_ref, c_ref, dt_ref, cwx_ref, cwb_ref, cwc_ref, cbx_ref, cbb_ref, cbc_ref,
              dtb_ref, alog_ref, dsk_ref, ng_ref, e_ref, o_ref, tx_ref, tb_ref, tc_ref, st_ref):
    L = SSD_CHUNK
    N = SSD_STATE
    GW = SSD_HEADS_PER_GROUP * SSD_HEAD_DIM

    @pl.when(pl.program_id(1) == 0)
    def _():
        tx_ref[...] = jnp.zeros_like(tx_ref)
        tb_ref[...] = jnp.zeros_like(tb_ref)
        tc_ref[...] = jnp.zeros_like(tc_ref)
        st_ref[...] = jnp.zeros_like(st_ref)

    xs = _silu(_causal_conv(x_ref[...].astype(F32), tx_ref, cwx_ref[...], cbx_ref[...], SSD_CONV))
    bm = _silu(_causal_conv(b_ref[...].astype(F32), tb_ref, cwb_ref[...], cbb_ref[...], SSD_CONV))
    cm = _silu(_causal_conv(c_ref[...].astype(F32), tc_ref, cwc_ref[...], cbc_ref[...], SSD_CONV))

    dt = _softplus(dt_ref[...] + dtb_ref[...])
    da = dt * (-jnp.exp(alog_ref[...]))
    row = lax.broadcasted_iota(jnp.int32, (L, L), 0)
    col = lax.broadcasted_iota(jnp.int32, (L, L), 1)
    causal = row >= col
    cs = _split_dot(da, causal.astype(BF16), 3, x_on_left=False)
    cs_t = cs.T
    cs_last = cs[L - 1:L, :]
    expand = e_ref[...]
    dt_e = _split_dot(dt, expand, 2)
    ecs_e = _split_dot(jnp.exp(cs), expand, 2)
    toend_e = _split_dot(jnp.exp(cs_last - cs), expand, 2)

    xdt = xs * dt_e
    bm_b = bm.astype(BF16)
    cm_b = cm.astype(BF16)
    xte_b = (xdt * toend_e).astype(BF16)
    prev = st_ref[...]
    prev_b = prev.astype(BF16)
    low_half = lax.broadcasted_iota(jnp.int32, (L, LANES), 1) < SSD_HEAD_DIM

    y_diag, y_off, new_state = [], [], []
    for g in range(SSD_GROUPS):
        cg = cm_b[:, g * N:(g + 1) * N]
        bg = bm_b[:, g * N:(g + 1) * N]
        cb = _dot_nt(cg, bg)
        for pair in range(SSD_HEADS_PER_GROUP // 2):
            h0 = g * SSD_HEADS_PER_GROUP + 2 * pair
            mats = []
            for h in (h0, h0 + 1):
                diff = cs[:, h:h + 1] - cs_t[h:h + 1, :]
                decay = jnp.exp(jnp.where(causal, diff, -jnp.inf))
                mats.append((cb * decay).astype(BF16))
            xp = xdt[:, h0 * SSD_HEAD_DIM:(h0 + 2) * SSD_HEAD_DIM]
            rhs = jnp.concatenate([jnp.where(low_half, xp, 0.0), jnp.where(low_half, 0.0, xp)], axis=0)
            y_diag.append(_dot(jnp.concatenate(mats, axis=1), rhs.astype(BF16)))
        y_off.append(_dot(cg, prev_b[:, g * GW:(g + 1) * GW]))
        new_state.append(_dot_tn(bg, xte_b[:, g * GW:(g + 1) * GW]))

    st_ref[...] = prev * ecs_e[L - 1:L, :] + jnp.concatenate(new_state, axis=1)
    y = (jnp.concatenate(y_diag, axis=1) + jnp.concatenate(y_off, axis=1) * ecs_e + xs * dsk_ref[...])
    y = y * _silu(z_ref[...].astype(F32))
    o_ref[...] = (_rms_scale(y) * ng_ref[...]).astype(o_ref.dtype)


def _ssd_mixer(proj, dt_raw, conv_w, conv_b, dt_bias, a_log, d_skip, norm_g, bsz, seq):
    L = SSD_CHUNK
    nc = seq // L
    W = SSD_WIDTH
    pad = LANES - SSD_HEADS
    cwx, cwb, cwc = conv_w[:, :W], conv_w[:, W:W + SSD_BC], conv_w[:, W + SSD_BC:]
    cb2 = conv_b.reshape(1, -1)
    cbx, cbb, cbc = cb2[:, :W], cb2[:, W:W + SSD_BC], cb2[:, W + SSD_BC:]
    dtb = jnp.pad(dt_bias, (0, pad)).reshape(1, LANES)
    alog = jnp.pad(a_log, (0, pad)).reshape(1, LANES)
    dsk = jnp.repeat(d_skip, SSD_HEAD_DIM).reshape(1, W)
    head_of_channel = jnp.arange(W, dtype=jnp.int32) // SSD_HEAD_DIM
    expand = (jnp.arange(LANES, dtype=jnp.int32)[:, None] == head_of_channel[None, :]).astype(BF16)

    def rows(b, c):
        return b * nc + c

    def const(shape):
        return pl.BlockSpec(shape, lambda b, c: (0, 0))

    zcol = 3 * SB_WIDTH // W
    bcol = (3 * SB_WIDTH + 2 * W) // SSD_BC
    est = 2 * (L * AB_MAIN * 2 + L * LANES * 4) + 2 * LANES * W * 2 + N_STATE_BYTES + 40 * L * W * 4
    return pl.pallas_call(
        _ssd_body,
        grid=(bsz, nc),
        in_specs=[pl.BlockSpec((L, W), lambda b, c: (rows(b, c), zcol)),
                  pl.BlockSpec((L, W), lambda b, c: (rows(b, c), zcol + 1)),
                  pl.BlockSpec((L, SSD_BC), lambda b, c: (rows(b, c), bcol)),
                  pl.BlockSpec((L, SSD_BC), lambda b, c: (rows(b, c), bcol + 1)),
                  pl.BlockSpec((L, LANES), lambda b, c: (rows(b, c), 0)),
                  const((SSD_CONV, W)), const((SSD_CONV, SSD_BC)), const((SSD_CONV, SSD_BC)),
                  const((1, W)), const((1, SSD_BC)), const((1, SSD_BC)),
                  const((1, LANES)), const((1, LANES)), const((1, W)), const((1, W)),
                  const((LANES, W))],
        out_specs=pl.BlockSpec((L, W), lambda b, c: (rows(b, c), 0)),
        out_shape=jax.ShapeDtypeStruct((bsz * seq, W), BF16),
        scratch_shapes=[pltpu.VMEM((HALO, W), F32), pltpu.VMEM((HALO, SSD_BC), F32),
                        pltpu.VMEM((HALO, SSD_BC), F32), pltpu.VMEM((SSD_STATE, W), F32)],
        compiler_params=_params(("arbitrary", "arbitrary"), est),
        name="ssd_mixer",
    )(proj, proj, proj, proj, dt_raw, cwx, cwb, cwc, cbx, cbb, cbc, dtb, alog, dsk,
      norm_g.reshape(1, W), expand)


N_STATE_BYTES = SSD_STATE * SSD_WIDTH * 4


def _lru_body(g_ref, x_ref, cw_ref, cb_ref, wa_ref, ba_ref, wi_ref, bi_ref, lam_ref, o_ref, tail_ref, h_ref):
    tt = LRU_TT

    @pl.when(pl.program_id(1) == 0)
    def _():
        tail_ref[...] = jnp.zeros_like(tail_ref)
        h_ref[...] = jnp.zeros_like(h_ref)

    xc = _causal_conv(x_ref[...].astype(F32), tail_ref, cw_ref[...], cb_ref[...], LRU_CONV)
    xc_b = xc.astype(BF16)
    pre_a, pre_i = [], []
    for blk in range(LRU_BLOCKS):
        xb = xc_b[:, blk * LRU_BLOCK:(blk + 1) * LRU_BLOCK]
        pre_a.append(_dot(xb, wa_ref[blk]))
        pre_i.append(_dot(xb, wi_ref[blk]))
    r = _sigmoid(jnp.concatenate(pre_a, axis=1) + ba_ref[...])
    gate_i = _sigmoid(jnp.concatenate(pre_i, axis=1) + bi_ref[...])
    log_a = (-LRU_C * r) * _softplus(-lam_ref[...])
    a = jnp.exp(log_a)
    u = jnp.sqrt(-jnp.tanh(log_a) * (a * a + 1.0)) * (gate_i * xc)

    d = 1
    while d < tt:
        u = u + a * _shift_rows(u, d, 0.0)
        a = a * _shift_rows(a, d, 1.0)
        d *= 2
    h = u + a * h_ref[...]
    h_ref[...] = h[tt - 1:tt, :]
    o_ref[...] = (h * _gelu_tanh(g_ref[...].astype(F32))).astype(o_ref.dtype)


def _lru_mixer(proj, conv_w, conv_b, w_a, b_a, w_i, b_i, lam, bsz, seq):
    W = LRU_WIDTH
    nt = seq // LRU_TT

    def const2(shape):
        return pl.BlockSpec(shape, lambda b, t: (0, 0))

    def const3(shape):
        return pl.BlockSpec(shape, lambda b, t: (0, 0, 0))

    est = 4 * LRU_TT * W * 2 + 4 * LRU_BLOCKS * LRU_BLOCK * LRU_BLOCK * 2 + 30 * LRU_TT * W * 4
    return pl.pallas_call(
        _lru_body,
        grid=(bsz, nt),
        in_specs=[pl.BlockSpec((LRU_TT, W), lambda b, t: (b * nt + t, 0)),
                  pl.BlockSpec((LRU_TT, W), lambda b, t: (b * nt + t, 1)),
                  const2((LRU_CONV, W)), const2((1, W)),
                  const3((LRU_BLOCKS, LRU_BLOCK, LRU_BLOCK)), const2((1, W)),
                  const3((LRU_BLOCKS, LRU_BLOCK, LRU_BLOCK)), const2((1, W)),
                  const2((1, W))],
        out_specs=pl.BlockSpec((LRU_TT, W), lambda b, t: (b * nt + t, 0)),
        out_shape=jax.ShapeDtypeStruct((bsz * seq, W), BF16),
        scratch_shapes=[pltpu.VMEM((HALO, W), F32), pltpu.VMEM((1, W), F32)],
        compiler_params=_params(("arbitrary", "arbitrary"), est),
        name="rglru_mixer",
    )(proj, proj, conv_w, conv_b.reshape(1, W), w_a.astype(BF16), b_a.reshape(1, W),
      w_i.astype(BF16), b_i.reshape(1, W), lam.reshape(1, W))


def _s5_body(u_ref, wb_ref, wc_ref, pwr_ref, pwi_ref, dsk_ref, wg_ref, bg_ref, o_ref, hr_ref, hi_ref):
    tt = S5_CHUNK
    SW = S5_STATE_WIDTH

    @pl.when(pl.program_id(1) == 0)
    def _():
        hr_ref[...] = jnp.zeros_like(hr_ref)
        hi_ref[...] = jnp.zeros_like(hi_ref)

    u_b = u_ref[...]
    bu = _dot(u_b, wb_ref[...])
    sr, si = bu[:, :SW], bu[:, SW:]
    pwr = pwr_ref[...]
    pwi = pwi_ref[...]
    d = 1
    while d < tt:
        ar, ai = pwr[d - 1:d, :], pwi[d - 1:d, :]
        sr_s, si_s = _shift_rows(sr, d, 0.0), _shift_rows(si, d, 0.0)
        sr, si = sr + (ar * sr_s - ai * si_s), si + (ar * si_s + ai * sr_s)
        d *= 2
    hr, hi = hr_ref[...], hi_ref[...]
    sr, si = sr + (pwr * hr - pwi * hi), si + (pwr * hi + pwi * hr)
    hr_ref[...] = sr[tt - 1:tt, :]
    hi_ref[...] = si[tt - 1:tt, :]
    s_cat = jnp.concatenate([sr, si], axis=1).astype(BF16)
    y = _gelu_tanh(_dot(s_cat, wc_ref[...]) + dsk_ref[...] * u_b.astype(F32))
    o_ref[...] = (y * _sigmoid(_dot(y.astype(BF16), wg_ref[...]) + bg_ref[...])).astype(o_ref.dtype)


def _s5_tables(log_dt, a_re, a_im, b_re, b_im, c_re, c_im):
    G, GC, N, L = S5_GROUPS, S5_GROUP_CH, S5_STATE, S5_CHUNK
    dt = jnp.exp(log_dt)[:, None]
    lr = jnp.minimum(a_re, -1e-4)
    li = a_im
    mag = jnp.exp(dt * lr)
    ab_i = mag * jnp.sin(dt * li)
    am1_r = jnp.expm1(dt * lr) * jnp.cos(dt * li) - 2.0 * jnp.square(jnp.sin(0.5 * dt * li))
    den = lr * lr + li * li
    f_r = (am1_r * lr + ab_i * li) / den
    f_i = (ab_i * lr - am1_r * li) / den
    bb_r = f_r[..., None] * b_re - f_i[..., None] * b_im
    bb_i = f_r[..., None] * b_im + f_i[..., None] * b_re
    eye = jnp.eye(G, dtype=F32)
    wb_r = jnp.einsum('gnc,gh->gchn', bb_r, eye).reshape(G * GC, G * N)
    wb_i = jnp.einsum('gnc,gh->gchn', bb_i, eye).reshape(G * GC, G * N)
    wc_r = jnp.einsum('gcn,gh->gnhc', c_re, eye).reshape(G * N, G * GC)
    wc_i = jnp.einsum('gcn,gh->gnhc', c_im, eye).reshape(G * N, G * GC)
    wb = jnp.concatenate([wb_r, wb_i], axis=1).astype(BF16)
    wc = jnp.concatenate([wc_r, -wc_i], axis=0).astype(BF16)
    powers = jnp.arange(1, L + 1, dtype=F32)[:, None, None] * dt[None]
    pw_mag = jnp.exp(powers * lr)
    pw_r = (pw_mag * jnp.cos(powers * li)).reshape(L, G * N)
    pw_i = (pw_mag * jnp.sin(powers * li)).reshape(L, G * N)
    return wb, wc, pw_r, pw_i


def _s5_mixer(proj, log_dt, a_re, a_im, b_re, b_im, c_re, c_im, d_skip, w_glu, b_glu, bsz, seq):
    W = S5_WIDTH
    SW = S5_STATE_WIDTH
    tt = S5_CHUNK
    nt = seq // tt
    wb, wc, pw_r, pw_i = _s5_tables(log_dt, a_re, a_im, b_re, b_im, c_re, c_im)

    def const(shape):
        return pl.BlockSpec(shape, lambda b, t: (0, 0))

    ucol = 2 * LRU_WIDTH // W
    est = 8 * W * SW * 2 + 4 * tt * SW * 4 + 2 * W * W * 2 + 24 * tt * SW * 4
    return pl.pallas_call(
        _s5_body,
        grid=(bsz, nt),
        in_specs=[pl.BlockSpec((tt, W), lambda b, t: (b * nt + t, ucol)),
                  const((W, 2 * SW)), const((2 * SW, W)), const((tt, SW)), const((tt, SW)),
                  const((1, W)), const((W, W)), const((1, W))],
        out_specs=pl.BlockSpec((tt, W), lambda b, t: (b * nt + t, 0)),
        out_shape=jax.ShapeDtypeStruct((bsz * seq, W), BF16),
        scratch_shapes=[pltpu.VMEM((1, SW), F32), pltpu.VMEM((1, SW), F32)],
        compiler_params=_params(("arbitrary", "arbitrary"), est),
        name="s5_mixer",
    )(proj, wb, wc, pw_r, pw_i, d_skip.reshape(1, W), w_glu.astype(BF16), b_glu.reshape(1, W))


def _out_proj_body(h_ref, a_ref, b_ref, wa_ref, wb_ref, o_ref):
    o_ref[...] = h_ref[...] + _dot(a_ref[...], wa_ref[...]) + _dot(b_ref[...], wb_ref[...])


def _out_proj(h, a, b, w):
    t, d = h.shape
    ka, kb = a.shape[1], b.shape[1]
    wa, wb = w[:ka], w[ka:]
    est = 4 * OUT_TM * OUT_TN * 4 + 2 * OUT_TM * (ka + kb) * 2 + 2 * (ka + kb) * OUT_TN * 2 + 2 * OUT_TM * OUT_TN * 4
    return pl.pallas_call(
        _out_proj_body,
        grid=(t // OUT_TM, d // OUT_TN),
        in_specs=[pl.BlockSpec((OUT_TM, OUT_TN), lambda i, j: (i, j)),
                  pl.BlockSpec((OUT_TM, ka), lambda i, j: (i, 0)),
                  pl.BlockSpec((OUT_TM, kb), lambda i, j: (i, 0)),
                  pl.BlockSpec((ka, OUT_TN), lambda i, j: (0, j)),
                  pl.BlockSpec((kb, OUT_TN), lambda i, j: (0, j))],
        out_specs=pl.BlockSpec((OUT_TM, OUT_TN), lambda i, j: (i, j)),
        out_shape=jax.ShapeDtypeStruct((t, d), F32),
        compiler_params=_params(("arbitrary", "arbitrary"), est),
        name="out_proj",
    )(h, a, b, wa, wb)


def _ffn_body(h_ref, g_ref, wv_ref, wg_ref, cw_ref, cb_ref, wd_ref, o_ref, yn_ref, tail_ref, *, tiles_per_seq):
    i = pl.program_id(0)
    j = pl.program_id(1)
    tm = FFN_TM

    @pl.when(j == 0)
    def _():
        x = h_ref[...]
        yn_ref[...] = (_rms_scale(x) * g_ref[...]).astype(BF16)
        o_ref[...] = x

    yn = yn_ref[...]
    val = _dot(yn, wv_ref[...])
    gate = _dot(yn, wg_ref[...])
    tail = jnp.where(i % tiles_per_seq == 0, 0.0, tail_ref[j])
    tail_ref[j] = gate[tm - HALO:, :]
    ext = jnp.concatenate([tail, gate], axis=0)
    w = cw_ref[...]
    cv = cb_ref[...] + w[2:3, :] * gate
    for k in range(FFN_CONV - 1):
        off = HALO - (FFN_CONV - 1) + k
        cv = cv + w[k:k + 1, :] * ext[off:off + tm, :]
    act = (_silu(cv) * val).astype(BF16)
    o_ref[...] += _dot(act, wd_ref[...])


def _conv_ffn(h, g, w_up, conv_w, conv_b, w_down, seq):
    t, d = h.shape
    nj = D_FF // FFN_TF
    est = (4 * FFN_TM * d * 4 + FFN_TM * d * 2 + 4 * d * FFN_TF * 2 + 2 * FFN_TF * d * 2
           + 8 * FFN_TM * FFN_TF * 4)
    return pl.pallas_call(
        functools.partial(_ffn_body, tiles_per_seq=seq // FFN_TM),
        grid=(t // FFN_TM, nj),
        in_specs=[pl.BlockSpec((FFN_TM, d), lambda i, j: (i, 0)),
                  pl.BlockSpec((1, d), lambda i, j: (0, 0)),
                  pl.BlockSpec((d, FFN_TF), lambda i, j: (0, j)),
                  pl.BlockSpec((d, FFN_TF), lambda i, j: (0, nj + j)),
                  pl.BlockSpec((FFN_CONV, FFN_TF), lambda i, j: (0, j)),
                  pl.BlockSpec((1, FFN_TF), lambda i, j: (0, j)),
                  pl.BlockSpec((FFN_TF, d), lambda i, j: (j, 0))],
        out_specs=pl.BlockSpec((FFN_TM, d), lambda i, j: (i, 0)),
        out_shape=jax.ShapeDtypeStruct((t, d), F32),
        scratch_shapes=[pltpu.VMEM((FFN_TM, d), BF16), pltpu.VMEM((nj, HALO, FFN_TF), F32)],
        compiler_params=_params(("arbitrary", "arbitrary"), est),
        name="conv_ffn",
    )(h, g.reshape(1, d), w_up, w_up, conv_w, conv_b.reshape(1, D_FF), w_down)


def _final_norm_body(h_ref, g_ref, o_ref):
    o_ref[...] = _rms_scale(h_ref[...]) * g_ref[...]


def _final_norm(h, g):
    t, d = h.shape
    return pl.pallas_call(
        _final_norm_body,
        grid=(t // NORM_TM,),
        in_specs=[pl.BlockSpec((NORM_TM, d), lambda i: (i, 0)), pl.BlockSpec((1, d), lambda i: (0, 0))],
        out_specs=pl.BlockSpec((NORM_TM, d), lambda i: (i, 0)),
        out_shape=jax.ShapeDtypeStruct((t, d), F32),
        compiler_params=_params(("arbitrary",), 4 * NORM_TM * d * 4),
        name="final_norm",
    )(h, g.reshape(1, d))


def _attn_ssd_layer(h, g, w_in, conv_w, conv_b, dt_bias, a_log, d_skip, ssd_norm, w_out, bsz, seq):
    w_main = w_in[:, :AB_MAIN].astype(BF16)
    w_dt = jnp.pad(w_in[:, AB_MAIN:], ((0, 0), (0, LANES - SSD_HEADS))).astype(BF16)
    proj, dt_raw = _norm_proj(h, g, w_main, w_dt)
    o_a = _sb_attention(proj, bsz, seq)
    o_b = _ssd_mixer(proj, dt_raw, conv_w, conv_b, dt_bias, a_log, d_skip, ssd_norm, bsz, seq)
    return _out_proj(h, o_a, o_b, w_out.astype(BF16))


def _lru_s5_layer(h, g, w_in, conv_w, conv_b, w_a, b_a, w_i, b_i, lam, log_dt, a_re, a_im, b_re, b_im,
                  c_re, c_im, d_skip, w_glu, b_glu, w_out, bsz, seq):
    proj = _norm_proj(h, g, w_in.astype(BF16))
    o_c = _lru_mixer(proj, conv_w, conv_b, w_a, b_a, w_i, b_i, lam, bsz, seq)
    o_d = _s5_mixer(proj, log_dt, a_re, a_im, b_re, b_im, c_re, c_im, d_skip, w_glu, b_glu, bsz, seq)
    return _out_proj(h, o_c, o_d, w_out.astype(BF16))


def kernel(x, norm_mix, norm_ffn, norm_final, ab_w_in, ab_conv_w, ab_conv_b, ab_dt_bias, ab_a_log, ab_d_skip, ab_ssd_norm, ab_w_out, cd_w_in, cd_conv_w, cd_conv_b, cd_w_a, cd_b_a, cd_w_i, cd_b_i, cd_lambda, cd_log_dt, cd_a_re, cd_a_im, cd_b_re, cd_b_im, cd_c_re, cd_c_im, cd_d_skip, cd_w_glu, cd_b_glu, cd_w_out, ffn_w_up, ffn_conv_w, ffn_conv_b, ffn_w_down):
    bsz, seq, d = x.shape
    depth = norm_mix.shape[0]
    h = x.reshape(bsz * seq, d)
    for layer in range(depth):
        j = layer // 2
        if layer % 2 == 0:
            h = _attn_ssd_layer(h, norm_mix[layer], ab_w_in[j], ab_conv_w[j], ab_conv_b[j], ab_dt_bias[j],
                                ab_a_log[j], ab_d_skip[j], ab_ssd_norm[j], ab_w_out[j], bsz, seq)
        else:
            h = _lru_s5_layer(h, norm_mix[layer], cd_w_in[j], cd_conv_w[j], cd_conv_b[j], cd_w_a[j], cd_b_a[j],
                              cd_w_i[j], cd_b_i[j], cd_lambda[j], cd_log_dt[j], cd_a_re[j], cd_a_im[j],
                              cd_b_re[j], cd_b_im[j], cd_c_re[j], cd_c_im[j], cd_d_skip[j], cd_w_glu[j],
                              cd_b_glu[j], cd_w_out[j], bsz, seq)
        h = _conv_ffn(h, norm_ffn[layer], ffn_w_up[layer].astype(BF16), ffn_conv_w[layer], ffn_conv_b[layer],
                      ffn_w_down[layer].astype(BF16), seq)
    return _final_norm(h, norm_final).reshape(bsz, seq, d)
```

```python
import functools

import jax
import jax.numpy as jnp
from jax import lax
from jax.experimental import pallas as pl
from jax.experimental.pallas import tpu as pltpu

F32 = jnp.float32
BF16 = jnp.bfloat16

D_MODEL = 2048
SB_HEADS = 4
SB_HEAD_DIM = 128
SB_WIDTH = SB_HEADS * SB_HEAD_DIM
SSD_HEADS = 24
SSD_HEAD_DIM = 64
SSD_WIDTH = SSD_HEADS * SSD_HEAD_DIM
SSD_GROUPS = 4
SSD_HEADS_PER_GROUP = SSD_HEADS // SSD_GROUPS
SSD_STATE = 128
SSD_CONV = 4
SSD_CHUNK = 128
SSD_BC = SSD_GROUPS * SSD_STATE
AB_MAIN = 3 * SB_WIDTH + 2 * SSD_WIDTH + 2 * SSD_BC
LRU_WIDTH = 1536
LRU_BLOCK = 128
LRU_BLOCKS = LRU_WIDTH // LRU_BLOCK
LRU_CONV = 4
LRU_C = 8.0
S5_GROUPS = 32
S5_GROUP_CH = 16
S5_WIDTH = S5_GROUPS * S5_GROUP_CH
S5_STATE = 64
S5_CHUNK = 128
S5_STATE_WIDTH = S5_GROUPS * S5_STATE
CD_IN = 2 * LRU_WIDTH + S5_WIDTH
D_FF = 4096
FFN_CONV = 3
EPS = 1e-6

LANES = 128
SUBLANES = 8
VMEM_CAP_BYTES = 60000 * 1024

PROJ_TM = 1024
PROJ_TN = 512
OUT_TM = 1024
OUT_TN = 1024
FFN_TM = 512
FFN_TF = 512
FFN_SUB = 256
SB_TQ = 256
LRU_TT = 256
NORM_TM = 1024
HALO = SUBLANES


def _vmem_limit(est_bytes):
    return int(min(max(est_bytes * 5 // 4, 32 * 1024 * 1024), VMEM_CAP_BYTES))


def _params(semantics, est_bytes):
    return pltpu.CompilerParams(dimension_semantics=semantics, vmem_limit_bytes=_vmem_limit(est_bytes))


def _dot(a, b):
    return jnp.dot(a, b, preferred_element_type=F32)


def _dot_nt(a, b):
    return lax.dot_general(a, b, (((1,), (1,)), ((), ())), preferred_element_type=F32)


def _dot_tn(a, b):
    return lax.dot_general(a, b, (((0,), (0,)), ((), ())), preferred_element_type=F32)


def _split_dot(x, m, terms, x_on_left=True):
    acc = None
    r = x
    for _ in range(terms):
        p = r.astype(BF16)
        d = _dot(p, m) if x_on_left else _dot(m, p)
        acc = d if acc is None else acc + d
        r = r - p.astype(F32)
    return acc


def _softplus(x):
    return jnp.maximum(x, 0.0) + jnp.log(1.0 + jnp.exp(-jnp.abs(x)))


def _sigmoid(x):
    return 1.0 / (1.0 + jnp.exp(-x))


def _silu(x):
    return x / (1.0 + jnp.exp(-x))


def _gelu_tanh(x):
    return 0.5 * x * (1.0 + jnp.tanh(0.7978845608028654 * (x + 0.044715 * (x * x * x))))


def _rms_scale(x):
    return x * lax.rsqrt(jnp.mean(x * x, axis=-1, keepdims=True) + EPS)


def _shift_in_group(x, d, fill):
    rows, width = x.shape
    x3 = x.reshape(rows // SUBLANES, SUBLANES, width)
    sub = lax.broadcasted_iota(jnp.int32, x3.shape, 1)
    return jnp.where(sub >= d, pltpu.roll(x3, d, axis=1), fill).reshape(rows, width)


def _last_row_bcast(x):
    return jnp.broadcast_to(x[SUBLANES - 1:SUBLANES, :], x.shape)


def _causal_conv(cur, tail_ref, w, bias, taps):
    rows = cur.shape[0]
    ext = jnp.concatenate([tail_ref[...], cur], axis=0)
    tail_ref[...] = cur[rows - HALO:, :]
    y = bias + w[taps - 1:taps, :] * cur
    for k in range(taps - 1):
        off = HALO - (taps - 1) + k
        y = y + w[k:k + 1, :] * ext[off:off + rows, :]
    return y


def _norm_proj_body(*refs, has_aux):
    if has_aux:
        h_ref, g_ref, w_ref, wa_ref, o_ref, oa_ref, yn_ref = refs
    else:
        h_ref, g_ref, w_ref, o_ref, yn_ref = refs

    @pl.when(pl.program_id(1) == 0)
    def _():
        yn_ref[...] = (_rms_scale(h_ref[...]) * g_ref[...]).astype(BF16)
        if has_aux:
            oa_ref[...] = _dot(yn_ref[...], wa_ref[...])

    o_ref[...] = _dot(yn_ref[...], w_ref[...]).astype(o_ref.dtype)


def _norm_proj(h, g, w, w_aux=None):
    t, d = h.shape
    n = w.shape[1]
    has_aux = w_aux is not None
    in_specs = [pl.BlockSpec((PROJ_TM, d), lambda i, j: (i, 0)),
                pl.BlockSpec((1, d), lambda i, j: (0, 0)),
                pl.BlockSpec((d, PROJ_TN), lambda i, j: (0, j))]
    out_shape = [jax.ShapeDtypeStruct((t, n), BF16)]
    out_specs = [pl.BlockSpec((PROJ_TM, PROJ_TN), lambda i, j: (i, j))]
    args = [h, g.reshape(1, d), w]
    if has_aux:
        in_specs.append(pl.BlockSpec((d, LANES), lambda i, j: (0, 0)))
        out_shape.append(jax.ShapeDtypeStruct((t, LANES), F32))
        out_specs.append(pl.BlockSpec((PROJ_TM, LANES), lambda i, j: (i, 0)))
        args.append(w_aux)
    est = 2 * PROJ_TM * d * 4 + PROJ_TM * d * 2 + 4 * d * PROJ_TN + 4 * PROJ_TM * PROJ_TN + 4 * PROJ_TM * d
    out = pl.pallas_call(
        functools.partial(_norm_proj_body, has_aux=has_aux),
        grid=(t // PROJ_TM, n // PROJ_TN),
        in_specs=in_specs, out_specs=out_specs, out_shape=out_shape,
        scratch_shapes=[pltpu.VMEM((PROJ_TM, d), BF16)],
        compiler_params=_params(("arbitrary", "arbitrary"), est),
        name="norm_proj_aux" if has_aux else "norm_proj",
    )(*args)
    return out if has_aux else out[0]


def _sb_body(q_ref, k_ref, v_ref, o_ref):
    i = pl.program_id(1)
    tq = SB_TQ
    dh = SB_HEAD_DIM
    row = lax.broadcasted_iota(jnp.int32, (tq, tq), 0)
    col = lax.broadcasted_iota(jnp.int32, (tq, tq), 1)
    tri = (row >= col).astype(BF16)
    strict = col < row
    qscale = (dh ** -0.5) * 1.4426950408889634
    qs = [(q_ref[:, h * dh:(h + 1) * dh].astype(F32) * qscale).astype(BF16) for h in range(SB_HEADS)]

    def block(h, start, masked, nls, acc):
        k = k_ref[pl.ds(start, tq), h * dh:(h + 1) * dh]
        v = v_ref[pl.ds(start, tq), h * dh:(h + 1) * dh]
        zb = _dot_nt(qs[h], k)
        nl = jnp.maximum(zb, 0.0) + jnp.log2(1.0 + jnp.exp2(-jnp.abs(zb)))
        if masked:
            nl = jnp.where(strict, nl, 0.0)
        suffix = _dot(nl.astype(BF16), tri)
        w = jnp.exp2(zb - suffix - nls)
        if masked:
            w = jnp.where(strict, w, 0.0)
        acc = acc + _dot(w.astype(BF16), v)
        nls = nls + jnp.sum(nl, axis=-1, keepdims=True)
        return nls, acc

    def step(kb, masked, carry):
        start = pl.multiple_of(kb * tq, tq)
        return tuple(block(h, start, masked, *carry[h]) for h in range(SB_HEADS))

    init = tuple((jnp.zeros((tq, 1), F32), jnp.zeros((tq, dh), F32)) for _ in range(SB_HEADS))
    carry = step(i, True, init)
    carry = lax.fori_loop(0, i, lambda n, c: step(i - 1 - n, False, c), carry)
    o_ref[...] = jnp.concatenate([c[1] for c in carry], axis=1).astype(o_ref.dtype)


def _sb_attention(proj, bsz, seq):
    nq = seq // SB_TQ
    est = 2 * seq * SB_WIDTH * 2 + 64 * SB_TQ * SB_TQ * 4
    resident = dict(pipeline_mode=pl.Buffered(1))
    return pl.pallas_call(
        _sb_body,
        grid=(bsz, nq),
        in_specs=[pl.BlockSpec((SB_TQ, SB_WIDTH), lambda b, i: (b * nq + i, 0)),
                  pl.BlockSpec((seq, SB_WIDTH), lambda b, i: (b, 1), **resident),
                  pl.BlockSpec((seq, SB_WIDTH), lambda b, i: (b, 2), **resident)],
        out_specs=pl.BlockSpec((SB_TQ, SB_WIDTH), lambda b, i: (b * nq + i, 0)),
        out_shape=jax.ShapeDtypeStruct((bsz * seq, SB_WIDTH), BF16),
        compiler_params=_params(("arbitrary", "arbitrary"), est),
        name="sb_attention",
    )(proj, proj, proj)


def _ssd_body(z_ref, x_ref, b_ref, c_ref, dt_ref, cwx_ref, cwb_ref, cwc_ref, cbx_ref, cbb_ref, cbc_ref,
              dtb_ref, alog_ref, dsk_ref, ng_ref, e_ref, o_ref, tx_ref, tb_ref, tc_ref, st_ref):
    L = SSD_CHUNK
    N = SSD_STATE
    GW = SSD_HEADS_PER_GROUP * SSD_HEAD_DIM

    @pl.when(pl.program_id(1) == 0)
    def _():
        tx_ref[...] = jnp.zeros_like(tx_ref)
        tb_ref[...] = jnp.zeros_like(tb_ref)
        tc_ref[...] = jnp.zeros_like(tc_ref)
        st_ref[...] = jnp.zeros_like(st_ref)

    xs = _silu(_causal_conv(x_ref[...].astype(F32), tx_ref, cwx_ref[...], cbx_ref[...], SSD_CONV))
    bm = _silu(_causal_conv(b_ref[...].astype(F32), tb_ref, cwb_ref[...], cbb_ref[...], SSD_CONV))
    cm = _silu(_causal_conv(c_ref[...].astype(F32), tc_ref, cwc_ref[...], cbc_ref[...], SSD_CONV))

    dt = _softplus(dt_ref[...] + dtb_ref[...])
    da = dt * (-jnp.exp(alog_ref[...]))
    row = lax.broadcasted_iota(jnp.int32, (L, L), 0)
    col = lax.broadcasted_iota(jnp.int32, (L, L), 1)
    causal = row >= col
    cs = _split_dot(da, causal.astype(BF16), 3, x_on_left=False)
    cs_t = cs.T
    cs_last = cs[L - 1:L, :]
    expand = e_ref[...]
    dt_e = _split_dot(dt, expand, 2)
    ecs_e = _split_dot(jnp.exp(cs), expand, 2)
    toend_e = _split_dot(jnp.exp(cs_last - cs), expand, 2)

    xdt = xs * dt_e
    bm_b = bm.astype(BF16)
    cm_b = cm.astype(BF16)
    xte_b = (xdt * toend_e).astype(BF16)
    prev = st_ref[...]
    prev_b = prev.astype(BF16)
    low_half = lax.broadcasted_iota(jnp.int32, (L, LANES), 1) < SSD_HEAD_DIM

    y_diag, y_off, new_state = [], [], []
    for g in range(SSD_GROUPS):
        cg = cm_b[:, g * N:(g + 1) * N]
        bg = bm_b[:, g * N:(g + 1) * N]
        cb = _dot_nt(cg, bg)
        for pair in range(SSD_HEADS_PER_GROUP // 2):
            h0 = g * SSD_HEADS_PER_GROUP + 2 * pair
            mats = []
            for h in (h0, h0 + 1):
                diff = cs[:, h:h + 1] - cs_t[h:h + 1, :]
                decay = jnp.exp(jnp.where(causal, diff, -jnp.inf))
                mats.append((cb * decay).astype(BF16))
            xp = xdt[:, h0 * SSD_HEAD_DIM:(h0 + 2) * SSD_HEAD_DIM]
            rhs = jnp.concatenate([jnp.where(low_half, xp, 0.0), jnp.where(low_half, 0.0, xp)], axis=0)
            y_diag.append(_dot(jnp.concatenate(mats, axis=1), rhs.astype(BF16)))
        y_off.append(_dot(cg, prev_b[:, g * GW:(g + 1) * GW]))
        new_state.append(_dot_tn(bg, xte_b[:, g * GW:(g + 1) * GW]))

    st_ref[...] = prev * ecs_e[L - 1:L, :] + jnp.concatenate(new_state, axis=1)
    y = (jnp.concatenate(y_diag, axis=1) + jnp.concatenate(y_off, axis=1) * ecs_e + xs * dsk_ref[...])
    y = y * _silu(z_ref[...].astype(F32))
    o_ref[...] = (_rms_scale(y) * ng_ref[...]).astype(o_ref.dtype)


def _ssd_mixer(proj, dt_raw, conv_w, conv_b, dt_bias, a_log, d_skip, norm_g, bsz, seq):
    L = SSD_CHUNK
    nc = seq // L
    W = SSD_WIDTH
    pad = LANES - SSD_HEADS
    cwx, cwb, cwc = conv_w[:, :W], conv_w[:, W:W + SSD_BC], conv_w[:, W + SSD_BC:]
    cb2 = conv_b.reshape(1, -1)
    cbx, cbb, cbc = cb2[:, :W], cb2[:, W:W + SSD_BC], cb2[:, W + SSD_BC:]
    dtb = jnp.pad(dt_bias, (0, pad)).reshape(1, LANES)
    alog = jnp.pad(a_log, (0, pad)).reshape(1, LANES)
    dsk = jnp.repeat(d_skip, SSD_HEAD_DIM).reshape(1, W)
    head_of_channel = jnp.arange(W, dtype=jnp.int32) // SSD_HEAD_DIM
    expand = (jnp.arange(LANES, dtype=jnp.int32)[:, None] == head_of_channel[None, :]).astype(BF16)

    def rows(b, c):
        return b * nc + c

    def const(shape):
        return pl.BlockSpec(shape, lambda b, c: (0, 0))

    zcol = 3 * SB_WIDTH // W
    bcol = (3 * SB_WIDTH + 2 * W) // SSD_BC
    est = 2 * (L * AB_MAIN * 2 + L * LANES * 4) + 2 * LANES * W * 2 + N_STATE_BYTES + 40 * L * W * 4
    return pl.pallas_call(
        _ssd_body,
        grid=(bsz, nc),
        in_specs=[pl.BlockSpec((L, W), lambda b, c: (rows(b, c), zcol)),
                  pl.BlockSpec((L, W), lambda b, c: (rows(b, c), zcol + 1)),
                  pl.BlockSpec((L, SSD_BC), lambda b, c: (rows(b, c), bcol)),
                  pl.BlockSpec((L, SSD_BC), lambda b, c: (rows(b, c), bcol + 1)),
                  pl.BlockSpec((L, LANES), lambda b, c: (rows(b, c), 0)),
                  const((SSD_CONV, W)), const((SSD_CONV, SSD_BC)), const((SSD_CONV, SSD_BC)),
                  const((1, W)), const((1, SSD_BC)), const((1, SSD_BC)),
                  const((1, LANES)), const((1, LANES)), const((1, W)), const((1, W)),
                  const((LANES, W))],
        out_specs=pl.BlockSpec((L, W), lambda b, c: (rows(b, c), 0)),
        out_shape=jax.ShapeDtypeStruct((bsz * seq, W), BF16),
        scratch_shapes=[pltpu.VMEM((HALO, W), F32), pltpu.VMEM((HALO, SSD_BC), F32),
                        pltpu.VMEM((HALO, SSD_BC), F32), pltpu.VMEM((SSD_STATE, W), F32)],
        compiler_params=_params(("arbitrary", "arbitrary"), est),
        name="ssd_mixer",
    )(proj, proj, proj, proj, dt_raw, cwx, cwb, cwc, cbx, cbb, cbc, dtb, alog, dsk,
      norm_g.reshape(1, W), expand)


N_STATE_BYTES = SSD_STATE * SSD_WIDTH * 4


def _lru_body(g_ref, x_ref, cw_ref, cb_ref, wa_ref, ba_ref, wi_ref, bi_ref, lam_ref, o_ref, tail_ref, h_ref):
    tt = LRU_TT

    @pl.when(pl.program_id(1) == 0)
    def _():
        tail_ref[...] = jnp.zeros_like(tail_ref)
        h_ref[...] = jnp.zeros_like(h_ref)

    xc = _causal_conv(x_ref[...].astype(F32), tail_ref, cw_ref[...], cb_ref[...], LRU_CONV)
    xc_b = xc.astype(BF16)
    pre_a, pre_i = [], []
    for blk in range(LRU_BLOCKS):
        xb = xc_b[:, blk * LRU_BLOCK:(blk + 1) * LRU_BLOCK]
        pre_a.append(_dot(xb, wa_ref[blk]))
        pre_i.append(_dot(xb, wi_ref[blk]))
    r = _sigmoid(jnp.concatenate(pre_a, axis=1) + ba_ref[...])
    gate_i = _sigmoid(jnp.concatenate(pre_i, axis=1) + bi_ref[...])
    log_a = (-LRU_C * r) * _softplus(-lam_ref[...])
    a = jnp.exp(log_a)
    m = -jnp.tanh(log_a) * (a * a + 1.0)
    u = jnp.where(m > 0.0, m * lax.rsqrt(m), 0.0) * (gate_i * xc)

    d = 1
    while d < SUBLANES:
        u = u + a * _shift_in_group(u, d, 0.0)
        a = a * _shift_in_group(a, d, 1.0)
        d *= 2
    carry = h_ref[...]
    groups = []
    for g in range(tt // SUBLANES):
        rows = slice(g * SUBLANES, (g + 1) * SUBLANES)
        groups.append(u[rows, :] + a[rows, :] * carry)
        carry = _last_row_bcast(groups[-1])
    h_ref[...] = carry
    h = jnp.concatenate(groups, axis=0)
    o_ref[...] = (h * _gelu_tanh(g_ref[...].astype(F32))).astype(o_ref.dtype)


def _lru_mixer(proj, conv_w, conv_b, w_a, b_a, w_i, b_i, lam, bsz, seq):
    W = LRU_WIDTH
    nt = seq // LRU_TT

    def const2(shape):
        return pl.BlockSpec(shape, lambda b, t: (0, 0))

    def const3(shape):
        return pl.BlockSpec(shape, lambda b, t: (0, 0, 0))

    est = 4 * LRU_TT * W * 2 + 4 * LRU_BLOCKS * LRU_BLOCK * LRU_BLOCK * 2 + 30 * LRU_TT * W * 4
    return pl.pallas_call(
        _lru_body,
        grid=(bsz, nt),
        in_specs=[pl.BlockSpec((LRU_TT, W), lambda b, t: (b * nt + t, 0)),
                  pl.BlockSpec((LRU_TT, W), lambda b, t: (b * nt + t, 1)),
                  const2((LRU_CONV, W)), const2((1, W)),
                  const3((LRU_BLOCKS, LRU_BLOCK, LRU_BLOCK)), const2((1, W)),
                  const3((LRU_BLOCKS, LRU_BLOCK, LRU_BLOCK)), const2((1, W)),
                  const2((1, W))],
        out_specs=pl.BlockSpec((LRU_TT, W), lambda b, t: (b * nt + t, 0)),
        out_shape=jax.ShapeDtypeStruct((bsz * seq, W), BF16),
        scratch_shapes=[pltpu.VMEM((HALO, W), F32), pltpu.VMEM((SUBLANES, W), F32)],
        compiler_params=_params(("arbitrary", "arbitrary"), est),
        name="rglru_mixer",
    )(proj, proj, conv_w, conv_b.reshape(1, W), w_a.astype(BF16), b_a.reshape(1, W),
      w_i.astype(BF16), b_i.reshape(1, W), lam.reshape(1, W))


def _s5_body(u_ref, wb_ref, wc_ref, pwr_ref, pwi_ref, dsk_ref, wg_ref, bg_ref, o_ref, hr_ref, hi_ref):
    tt = S5_CHUNK
    SW = S5_STATE_WIDTH

    @pl.when(pl.program_id(1) == 0)
    def _():
        hr_ref[...] = jnp.zeros_like(hr_ref)
        hi_ref[...] = jnp.zeros_like(hi_ref)

    u_b = u_ref[...]
    bu = _dot(u_b, wb_ref[...])
    sr, si = bu[:, :SW], bu[:, SW:]
    pwr = pwr_ref[...]
    pwi = pwi_ref[...]
    d = 1
    while d < SUBLANES:
        ar, ai = pwr[d - 1:d, :], pwi[d - 1:d, :]
        sr_s, si_s = _shift_in_group(sr, d, 0.0), _shift_in_group(si, d, 0.0)
        sr, si = sr + (ar * sr_s - ai * si_s), si + (ar * si_s + ai * sr_s)
        d *= 2
    cr, ci = hr_ref[...], hi_ref[...]
    groups_r, groups_i = [], []
    for g in range(tt // SUBLANES):
        rows = slice(g * SUBLANES, (g + 1) * SUBLANES)
        groups_r.append(sr[rows, :] + (pwr * cr - pwi * ci))
        groups_i.append(si[rows, :] + (pwr * ci + pwi * cr))
        cr, ci = _last_row_bcast(groups_r[-1]), _last_row_bcast(groups_i[-1])
    hr_ref[...] = cr
    hi_ref[...] = ci
    sr = jnp.concatenate(groups_r, axis=0)
    si = jnp.concatenate(groups_i, axis=0)
    s_cat = jnp.concatenate([sr, si], axis=1).astype(BF16)
    y = _gelu_tanh(_dot(s_cat, wc_ref[...]) + dsk_ref[...] * u_b.astype(F32))
    o_ref[...] = (y * _sigmoid(_dot(y.astype(BF16), wg_ref[...]) + bg_ref[...])).astype(o_ref.dtype)


def _s5_tables(log_dt, a_re, a_im, b_re, b_im, c_re, c_im):
    G, GC, N = S5_GROUPS, S5_GROUP_CH, S5_STATE
    dt = jnp.exp(log_dt)[:, None]
    lr = jnp.minimum(a_re, -1e-4)
    li = a_im
    mag = jnp.exp(dt * lr)
    ab_i = mag * jnp.sin(dt * li)
    am1_r = jnp.expm1(dt * lr) * jnp.cos(dt * li) - 2.0 * jnp.square(jnp.sin(0.5 * dt * li))
    den = lr * lr + li * li
    f_r = (am1_r * lr + ab_i * li) / den
    f_i = (ab_i * lr - am1_r * li) / den
    bb_r = f_r[..., None] * b_re - f_i[..., None] * b_im
    bb_i = f_r[..., None] * b_im + f_i[..., None] * b_re
    eye = jnp.eye(G, dtype=F32)
    wb_r = jnp.einsum('gnc,gh->gchn', bb_r, eye).reshape(G * GC, G * N)
    wb_i = jnp.einsum('gnc,gh->gchn', bb_i, eye).reshape(G * GC, G * N)
    wc_r = jnp.einsum('gcn,gh->gnhc', c_re, eye).reshape(G * N, G * GC)
    wc_i = jnp.einsum('gcn,gh->gnhc', c_im, eye).reshape(G * N, G * GC)
    wb = jnp.concatenate([wb_r, wb_i], axis=1).astype(BF16)
    wc = jnp.concatenate([wc_r, -wc_i], axis=0).astype(BF16)
    powers = jnp.arange(1, SUBLANES + 1, dtype=F32)[:, None, None] * dt[None]
    pw_mag = jnp.exp(powers * lr)
    pw_r = (pw_mag * jnp.cos(powers * li)).reshape(SUBLANES, G * N)
    pw_i = (pw_mag * jnp.sin(powers * li)).reshape(SUBLANES, G * N)
    return wb, wc, pw_r, pw_i


def _s5_mixer(proj, log_dt, a_re, a_im, b_re, b_im, c_re, c_im, d_skip, w_glu, b_glu, bsz, seq):
    W = S5_WIDTH
    SW = S5_STATE_WIDTH
    tt = S5_CHUNK
    nt = seq // tt
    wb, wc, pw_r, pw_i = _s5_tables(log_dt, a_re, a_im, b_re, b_im, c_re, c_im)

    def const(shape):
        return pl.BlockSpec(shape, lambda b, t: (0, 0))

    ucol = 2 * LRU_WIDTH // W
    est = 8 * W * SW * 2 + 4 * tt * SW * 4 + 2 * W * W * 2 + 24 * tt * SW * 4
    return pl.pallas_call(
        _s5_body,
        grid=(bsz, nt),
        in_specs=[pl.BlockSpec((tt, W), lambda b, t: (b * nt + t, ucol)),
                  const((W, 2 * SW)), const((2 * SW, W)), const((SUBLANES, SW)), const((SUBLANES, SW)),
                  const((1, W)), const((W, W)), const((1, W))],
        out_specs=pl.BlockSpec((tt, W), lambda b, t: (b * nt + t, 0)),
        out_shape=jax.ShapeDtypeStruct((bsz * seq, W), BF16),
        scratch_shapes=[pltpu.VMEM((SUBLANES, SW), F32), pltpu.VMEM((SUBLANES, SW), F32)],
        compiler_params=_params(("arbitrary", "arbitrary"), est),
        name="s5_mixer",
    )(proj, wb, wc, pw_r, pw_i, d_skip.reshape(1, W), w_glu.astype(BF16), b_glu.reshape(1, W))


def _out_proj_body(h_ref, a_ref, b_ref, wa_ref, wb_ref, o_ref):
    o_ref[...] = h_ref[...] + _dot(a_ref[...], wa_ref[...]) + _dot(b_ref[...], wb_ref[...])


def _out_proj(h, a, b, w):
    t, d = h.shape
    ka, kb = a.shape[1], b.shape[1]
    wa, wb = w[:ka], w[ka:]
    est = 4 * OUT_TM * OUT_TN * 4 + 2 * OUT_TM * (ka + kb) * 2 + 2 * (ka + kb) * OUT_TN * 2 + 2 * OUT_TM * OUT_TN * 4
    return pl.pallas_call(
        _out_proj_body,
        grid=(t // OUT_TM, d // OUT_TN),
        in_specs=[pl.BlockSpec((OUT_TM, OUT_TN), lambda i, j: (i, j)),
                  pl.BlockSpec((OUT_TM, ka), lambda i, j: (i, 0)),
                  pl.BlockSpec((OUT_TM, kb), lambda i, j: (i, 0)),
                  pl.BlockSpec((ka, OUT_TN), lambda i, j: (0, j)),
                  pl.BlockSpec((kb, OUT_TN), lambda i, j: (0, j))],
        out_specs=pl.BlockSpec((OUT_TM, OUT_TN), lambda i, j: (i, j)),
        out_shape=jax.ShapeDtypeStruct((t, d), F32),
        compiler_params=_params(("arbitrary", "arbitrary"), est),
        name="out_proj",
    )(h, a, b, wa, wb)


def _ffn_body(h_ref, g_ref, wv_ref, wg_ref, cw_ref, cb_ref, wd_ref, o_ref, yn_ref, tail_ref, *, tiles_per_seq):
    i = pl.program_id(0)
    j = pl.program_id(1)
    tm = FFN_TM

    @pl.when(j == 0)
    def _():
        x = h_ref[...]
        yn_ref[...] = (_rms_scale(x) * g_ref[...]).astype(BF16)
        o_ref[...] = x

    yn = yn_ref[...]
    first = i % tiles_per_seq == 0
    for c in range(FFN_TF // FFN_SUB):
        cols = slice(c * FFN_SUB, (c + 1) * FFN_SUB)
        val = _dot(yn, wv_ref[:, cols])
        gate = _dot(yn, wg_ref[:, cols])
        tail = jnp.where(first, 0.0, tail_ref[j, :, cols])
        tail_ref[j, :, cols] = gate[tm - HALO:, :]
        ext = jnp.concatenate([tail, gate], axis=0)
        w = cw_ref[:, cols]
        cv = cb_ref[:, cols] + w[2:3, :] * gate
        for k in range(FFN_CONV - 1):
            off = HALO - (FFN_CONV - 1) + k
            cv = cv + w[k:k + 1, :] * ext[off:off + tm, :]
        act = (_silu(cv) * val).astype(BF16)
        o_ref[...] += _dot(act, wd_ref[cols, :])


def _conv_ffn(h, g, w_up, conv_w, conv_b, w_down, seq):
    t, d = h.shape
    nj = D_FF // FFN_TF
    est = (4 * FFN_TM * d * 4 + FFN_TM * d * 2 + 4 * d * FFN_TF * 2 + 2 * FFN_TF * d * 2
           + 8 * FFN_TM * FFN_TF * 4)
    return pl.pallas_call(
        functools.partial(_ffn_body, tiles_per_seq=seq // FFN_TM),
        grid=(t // FFN_TM, nj),
        in_specs=[pl.BlockSpec((FFN_TM, d), lambda i, j: (i, 0)),
                  pl.BlockSpec((1, d), lambda i, j: (0, 0)),
                  pl.BlockSpec((d, FFN_TF), lambda i, j: (0, j)),
                  pl.BlockSpec((d, FFN_TF), lambda i, j: (0, nj + j)),
                  pl.BlockSpec((FFN_CONV, FFN_TF), lambda i, j: (0, j)),
                  pl.BlockSpec((1, FFN_TF), lambda i, j: (0, j)),
                  pl.BlockSpec((FFN_TF, d), lambda i, j: (j, 0))],
        out_specs=pl.BlockSpec((FFN_TM, d), lambda i, j: (i, 0)),
        out_shape=jax.ShapeDtypeStruct((t, d), F32),
        scratch_shapes=[pltpu.VMEM((FFN_TM, d), BF16), pltpu.VMEM((nj, HALO, FFN_TF), F32)],
        compiler_params=_params(("arbitrary", "arbitrary"), est),
        name="conv_ffn",
    )(h, g.reshape(1, d), w_up, w_up, conv_w, conv_b.reshape(1, D_FF), w_down)


def _final_norm_body(h_ref, g_ref, o_ref):
    o_ref[...] = _rms_scale(h_ref[...]) * g_ref[...]


def _final_norm(h, g):
    t, d = h.shape
    return pl.pallas_call(
        _final_norm_body,
        grid=(t // NORM_TM,),
        in_specs=[pl.BlockSpec((NORM_TM, d), lambda i: (i, 0)), pl.BlockSpec((1, d), lambda i: (0, 0))],
        out_specs=pl.BlockSpec((NORM_TM, d), lambda i: (i, 0)),
        out_shape=jax.ShapeDtypeStruct((t, d), F32),
        compiler_params=_params(("arbitrary",), 4 * NORM_TM * d * 4),
        name="final_norm",
    )(h, g.reshape(1, d))


def _attn_ssd_layer(h, g, w_in, conv_w, conv_b, dt_bias, a_log, d_skip, ssd_norm, w_out, bsz, seq):
    w_main = w_in[:, :AB_MAIN].astype(BF16)
    w_dt = jnp.pad(w_in[:, AB_MAIN:], ((0, 0), (0, LANES - SSD_HEADS))).astype(BF16)
    proj, dt_raw = _norm_proj(h, g, w_main, w_dt)
    o_a = _sb_attention(proj, bsz, seq)
    o_b = _ssd_mixer(proj, dt_raw, conv_w, conv_b, dt_bias, a_log, d_skip, ssd_norm, bsz, seq)
    return _out_proj(h, o_a, o_b, w_out.astype(BF16))


def _lru_s5_layer(h, g, w_in, conv_w, conv_b, w_a, b_a, w_i, b_i, lam, log_dt, a_re, a_im, b_re, b_im,
                  c_re, c_im, d_skip, w_glu, b_glu, w_out, bsz, seq):
    proj = _norm_proj(h, g, w_in.astype(BF16))
    o_c = _lru_mixer(proj, conv_w, conv_b, w_a, b_a, w_i, b_i, lam, bsz, seq)
    o_d = _s5_mixer(proj, log_dt, a_re, a_im, b_re, b_im, c_re, c_im, d_skip, w_glu, b_glu, bsz, seq)
    return _out_proj(h, o_c, o_d, w_out.astype(BF16))


def kernel(x, norm_mix, norm_ffn, norm_final, ab_w_in, ab_conv_w, ab_conv_b, ab_dt_bias, ab_a_log, ab_d_skip, ab_ssd_norm, ab_w_out, cd_w_in, cd_conv_w, cd_conv_b, cd_w_a, cd_b_a, cd_w_i, cd_b_i, cd_lambda, cd_log_dt, cd_a_re, cd_a_im, cd_b_re, cd_b_im, cd_c_re, cd_c_im, cd_d_skip, cd_w_glu, cd_b_glu, cd_w_out, ffn_w_up, ffn_conv_w, ffn_conv_b, ffn_w_down):
    bsz, seq, d = x.shape
    depth = norm_mix.shape[0]
    h = x.reshape(bsz * seq, d)
    for layer in range(depth):
        j = layer // 2
        if layer % 2 == 0:
            h = _attn_ssd_layer(h, norm_mix[layer], ab_w_in[j], ab_conv_w[j], ab_conv_b[j], ab_dt_bias[j],
                                ab_a_log[j], ab_d_skip[j], ab_ssd_norm[j], ab_w_out[j], bsz, seq)
        else:
            h = _lru_s5_layer(h, norm_mix[layer], cd_w_in[j], cd_conv_w[j], cd_conv_b[j], cd_w_a[j], cd_b_a[j],
                              cd_w_i[j], cd_b_i[j], cd_lambda[j], cd_log_dt[j], cd_a_re[j], cd_a_im[j],
                              cd_b_re[j], cd_b_im[j], cd_c_re[j], cd_c_im[j], cd_d_skip[j], cd_w_glu[j],
                              cd_b_glu[j], cd_w_out[j], bsz, seq)
        h = _conv_ffn(h, norm_ffn[layer], ffn_w_up[layer].astype(BF16), ffn_conv_w[layer], ffn_conv_b[layer],
                      ffn_w_down[layer].astype(BF16), seq)
    return _final_norm(h, norm_final).reshape(bsz, seq, d)
```

```python
import functools

import jax
import jax.numpy as jnp
from jax import lax
from jax.experimental import pallas as pl
from jax.experimental.pallas import tpu as pltpu

F32 = jnp.float32
BF16 = jnp.bfloat16

D_MODEL = 2048
SB_HEADS = 4
SB_HEAD_DIM = 128
SB_WIDTH = SB_HEADS * SB_HEAD_DIM
SSD_HEADS = 24
SSD_HEAD_DIM = 64
SSD_WIDTH = SSD_HEADS * SSD_HEAD_DIM
SSD_GROUPS = 4
SSD_HEADS_PER_GROUP = SSD_HEADS // SSD_GROUPS
SSD_STATE = 128
SSD_CONV = 4
SSD_CHUNK = 128
SSD_BC = SSD_GROUPS * SSD_STATE
AB_MAIN = 3 * SB_WIDTH + 2 * SSD_WIDTH + 2 * SSD_BC
LRU_WIDTH = 1536
LRU_BLOCK = 128
LRU_BLOCKS = LRU_WIDTH // LRU_BLOCK
LRU_CONV = 4
LRU_C = 8.0
S5_GROUPS = 32
S5_GROUP_CH = 16
S5_WIDTH = S5_GROUPS * S5_GROUP_CH
S5_STATE = 64
S5_CHUNK = 128
S5_STATE_WIDTH = S5_GROUPS * S5_STATE
CD_IN = 2 * LRU_WIDTH + S5_WIDTH
D_FF = 4096
FFN_CONV = 3
EPS = 1e-6

LANES = 128
SUBLANES = 8
VMEM_CAP_BYTES = 60000 * 1024

PROJ_TM = 512
PROJ_TN = 512
OUT_TM = 512
OUT_TN = 512
FFN_TM = 512
FFN_TF = 1024
FFN_SUB = 256
SB_TQ = 256
SB_DEAD_LOG2 = 160.0
LRU_TT = 256
HALO = SUBLANES


def _vmem_limit(est_bytes):
    return int(min(max(est_bytes * 5 // 4, 32 * 1024 * 1024), VMEM_CAP_BYTES))


def _params(semantics, est_bytes):
    return pltpu.CompilerParams(dimension_semantics=semantics, vmem_limit_bytes=_vmem_limit(est_bytes))


def _dot(a, b):
    return jnp.dot(a, b, preferred_element_type=F32)


def _dot_nt(a, b):
    return lax.dot_general(a, b, (((1,), (1,)), ((), ())), preferred_element_type=F32)


def _dot_tn(a, b):
    return lax.dot_general(a, b, (((0,), (0,)), ((), ())), preferred_element_type=F32)


def _split_dot(x, m, terms, x_on_left=True):
    acc = None
    r = x
    for _ in range(terms):
        p = r.astype(BF16)
        d = _dot(p, m) if x_on_left else _dot(m, p)
        acc = d if acc is None else acc + d
        r = r - p.astype(F32)
    return acc


def _softplus(x):
    return jnp.maximum(x, 0.0) + jnp.log(1.0 + jnp.exp(-jnp.abs(x)))


def _sigmoid(x):
    return 1.0 / (1.0 + jnp.exp(-x))


def _silu(x):
    return x / (1.0 + jnp.exp(-x))


def _gelu_tanh(x):
    return 0.5 * x * (1.0 + jnp.tanh(0.7978845608028654 * (x + 0.044715 * (x * x * x))))


def _rms_scale(x):
    return x * lax.rsqrt(jnp.mean(x * x, axis=-1, keepdims=True) + EPS)


def _shift_in_group(x, d, fill):
    rows, width = x.shape
    x3 = x.reshape(rows // SUBLANES, SUBLANES, width)
    sub = lax.broadcasted_iota(jnp.int32, x3.shape, 1)
    return jnp.where(sub >= d, pltpu.roll(x3, d, axis=1), fill).reshape(rows, width)


def _last_row_bcast(x):
    return jnp.broadcast_to(x[SUBLANES - 1:SUBLANES, :], x.shape)


def _causal_conv(cur, tail_ref, w, bias, taps):
    rows = cur.shape[0]
    ext = jnp.concatenate([tail_ref[...], cur], axis=0)
    tail_ref[...] = cur[rows - HALO:, :]
    y = bias + w[taps - 1:taps, :] * cur
    for k in range(taps - 1):
        off = HALO - (taps - 1) + k
        y = y + w[k:k + 1, :] * ext[off:off + rows, :]
    return y


def _norm_proj_body(*refs, has_aux):
    if has_aux:
        h_ref, g_ref, w_ref, wa_ref, o_ref, oa_ref = refs
    else:
        h_ref, g_ref, w_ref, o_ref = refs
    yn = (_rms_scale(h_ref[...]) * g_ref[...]).astype(BF16)
    for c in range(w_ref.shape[1] // PROJ_TN):
        cols = slice(c * PROJ_TN, (c + 1) * PROJ_TN)
        o_ref[:, cols] = _dot(yn, w_ref[:, cols]).astype(o_ref.dtype)
    if has_aux:
        oa_ref[...] = _dot(yn, wa_ref[...])


def _norm_proj(h, g, w, w_aux=None):
    t, d = h.shape
    n = w.shape[1]
    has_aux = w_aux is not None
    resident = dict(pipeline_mode=pl.Buffered(1))
    in_specs = [pl.BlockSpec((PROJ_TM, d), lambda i: (i, 0)),
                pl.BlockSpec((1, d), lambda i: (0, 0)),
                pl.BlockSpec((d, n), lambda i: (0, 0), **resident)]
    out_shape = [jax.ShapeDtypeStruct((t, n), BF16)]
    out_specs = [pl.BlockSpec((PROJ_TM, n), lambda i: (i, 0))]
    args = [h, g.reshape(1, d), w]
    if has_aux:
        in_specs.append(pl.BlockSpec((d, LANES), lambda i: (0, 0), **resident))
        out_shape.append(jax.ShapeDtypeStruct((t, LANES), F32))
        out_specs.append(pl.BlockSpec((PROJ_TM, LANES), lambda i: (i, 0)))
        args.append(w_aux)
    est = (d * (n + LANES) * 2 + 2 * PROJ_TM * d * 4 + 2 * PROJ_TM * (n * 2 + LANES * 4)
           + PROJ_TM * d * 2 + 2 * PROJ_TM * PROJ_TN * 4)
    out = pl.pallas_call(
        functools.partial(_norm_proj_body, has_aux=has_aux),
        grid=(t // PROJ_TM,),
        in_specs=in_specs, out_specs=out_specs, out_shape=out_shape,
        compiler_params=_params(("arbitrary",), est),
        name="norm_proj_aux" if has_aux else "norm_proj",
    )(*args)
    return out if has_aux else out[0]


def _sb_body(q_ref, k_ref, v_ref, o_ref):
    i = pl.program_id(1)
    tq = SB_TQ
    dh = SB_HEAD_DIM
    row = lax.broadcasted_iota(jnp.int32, (tq, tq), 0)
    col = lax.broadcasted_iota(jnp.int32, (tq, tq), 1)
    tri = (row >= col).astype(BF16)
    strict = col < row
    qscale = (dh ** -0.5) * 1.4426950408889634
    qs = [(q_ref[:, h * dh:(h + 1) * dh].astype(F32) * qscale).astype(BF16) for h in range(SB_HEADS)]

    def block(h, start, masked, nls, acc):
        k = k_ref[pl.ds(start, tq), h * dh:(h + 1) * dh]
        v = v_ref[pl.ds(start, tq), h * dh:(h + 1) * dh]
        zb = _dot_nt(qs[h], k)
        nl = jnp.maximum(zb, 0.0) + jnp.log2(1.0 + jnp.exp2(-jnp.abs(zb)))
        if masked:
            nl = jnp.where(strict, nl, 0.0)
        suffix = _dot(nl.astype(BF16), tri)
        w = jnp.exp2(zb - suffix - nls)
        if masked:
            w = jnp.where(strict, w, 0.0)
        acc = acc + _dot(w.astype(BF16), v)
        nls = nls + jnp.sum(nl, axis=-1, keepdims=True)
        return nls, acc

    def step(kb, masked, carry):
        start = pl.multiple_of(kb * tq, tq)
        return tuple(block(h, start, masked, *carry[h]) for h in range(SB_HEADS))

    def live(carry):
        least = functools.reduce(jnp.minimum, [c[0] for c in carry])
        return jnp.min(least) < SB_DEAD_LOG2

    def body(state):
        n, _, carry = state
        carry = step(i - 1 - n, False, carry)
        return n + 1, live(carry), carry

    init = tuple((jnp.zeros((tq, 1), F32), jnp.zeros((tq, dh), F32)) for _ in range(SB_HEADS))
    carry = step(i, True, init)
    _, _, carry = lax.while_loop(lambda s: jnp.logical_and(s[0] < i, s[1]), body,
                                 (jnp.int32(0), live(carry), carry))
    o_ref[...] = jnp.concatenate([c[1] for c in carry], axis=1).astype(o_ref.dtype)


def _sb_attention(proj, bsz, seq):
    nq = seq // SB_TQ
    est = 2 * seq * SB_WIDTH * 2 + 64 * SB_TQ * SB_TQ * 4
    resident = dict(pipeline_mode=pl.Buffered(1))
    return pl.pallas_call(
        _sb_body,
        grid=(bsz, nq),
        in_specs=[pl.BlockSpec((SB_TQ, SB_WIDTH), lambda b, i: (b * nq + i, 0)),
                  pl.BlockSpec((seq, SB_WIDTH), lambda b, i: (b, 1), **resident),
                  pl.BlockSpec((seq, SB_WIDTH), lambda b, i: (b, 2), **resident)],
        out_specs=pl.BlockSpec((SB_TQ, SB_WIDTH), lambda b, i: (b * nq + i, 0)),
        out_shape=jax.ShapeDtypeStruct((bsz * seq, SB_WIDTH), BF16),
        compiler_params=_params(("arbitrary", "arbitrary"), est),
        name="sb_attention",
    )(proj, proj, proj)


def _ssd_body(z_ref, x_ref, b_ref, c_ref, dt_ref, cwx_ref, cwb_ref, cwc_ref, cbx_ref, cbb_ref, cbc_ref,
              dtb_ref, alog_ref, dsk_ref, ng_ref, e_ref, o_ref, tx_ref, tb_ref, tc_ref, st_ref):
    L = SSD_CHUNK
    N = SSD_STATE
    GW = SSD_HEADS_PER_GROUP * SSD_HEAD_DIM

    @pl.when(pl.program_id(1) == 0)
    def _():
        tx_ref[...] = jnp.zeros_like(tx_ref)
        tb_ref[...] = jnp.zeros_like(tb_ref)
        tc_ref[...] = jnp.zeros_like(tc_ref)
        st_ref[...] = jnp.zeros_like(st_ref)

    xs = _silu(_causal_conv(x_ref[...].astype(F32), tx_ref, cwx_ref[...], cbx_ref[...], SSD_CONV))
    bm = _silu(_causal_conv(b_ref[...].astype(F32), tb_ref, cwb_ref[...], cbb_ref[...], SSD_CONV))
    cm = _silu(_causal_conv(c_ref[...].astype(F32), tc_ref, cwc_ref[...], cbc_ref[...], SSD_CONV))

    dt = _softplus(dt_ref[...] + dtb_ref[...])
    da = dt * (-jnp.exp(alog_ref[...]))
    row = lax.broadcasted_iota(jnp.int32, (L, L), 0)
    col = lax.broadcasted_iota(jnp.int32, (L, L), 1)
    causal = row >= col
    cs = _split_dot(da, causal.astype(BF16), 3, x_on_left=False)
    cs_t = cs.T
    cs_last = cs[L - 1:L, :]
    expand = e_ref[...]
    dt_e = _split_dot(dt, expand, 2)
    ecs_e = _split_dot(jnp.exp(cs), expand, 2)
    toend_e = _split_dot(jnp.exp(cs_last - cs), expand, 2)

    xdt = xs * dt_e
    bm_b = bm.astype(BF16)
    cm_b = cm.astype(BF16)
    xte_b = (xdt * toend_e).astype(BF16)
    prev = st_ref[...]
    prev_b = prev.astype(BF16)
    low_half = lax.broadcasted_iota(jnp.int32, (L, LANES), 1) < SSD_HEAD_DIM

    y_diag, y_off, new_state = [], [], []
    for g in range(SSD_GROUPS):
        cg = cm_b[:, g * N:(g + 1) * N]
        bg = bm_b[:, g * N:(g + 1) * N]
        cb = _dot_nt(cg, bg)
        for pair in range(SSD_HEADS_PER_GROUP // 2):
            h0 = g * SSD_HEADS_PER_GROUP + 2 * pair
            mats = []
            for h in (h0, h0 + 1):
                diff = cs[:, h:h + 1] - cs_t[h:h + 1, :]
                decay = jnp.exp(jnp.where(causal, diff, -jnp.inf))
                mats.append((cb * decay).astype(BF16))
            xp = xdt[:, h0 * SSD_HEAD_DIM:(h0 + 2) * SSD_HEAD_DIM]
            rhs = jnp.concatenate([jnp.where(low_half, xp, 0.0), jnp.where(low_half, 0.0, xp)], axis=0)
            y_diag.append(_dot(jnp.concatenate(mats, axis=1), rhs.astype(BF16)))
        y_off.append(_dot(cg, prev_b[:, g * GW:(g + 1) * GW]))
        new_state.append(_dot_tn(bg, xte_b[:, g * GW:(g + 1) * GW]))

    st_ref[...] = prev * ecs_e[L - 1:L, :] + jnp.concatenate(new_state, axis=1)
    y = (jnp.concatenate(y_diag, axis=1) + jnp.concatenate(y_off, axis=1) * ecs_e + xs * dsk_ref[...])
    y = y * _silu(z_ref[...].astype(F32))
    o_ref[...] = (_rms_scale(y) * ng_ref[...]).astype(o_ref.dtype)


def _ssd_mixer(proj, dt_raw, conv_w, conv_b, dt_bias, a_log, d_skip, norm_g, bsz, seq):
    L = SSD_CHUNK
    nc = seq // L
    W = SSD_WIDTH
    pad = LANES - SSD_HEADS
    cwx, cwb, cwc = conv_w[:, :W], conv_w[:, W:W + SSD_BC], conv_w[:, W + SSD_BC:]
    cb2 = conv_b.reshape(1, -1)
    cbx, cbb, cbc = cb2[:, :W], cb2[:, W:W + SSD_BC], cb2[:, W + SSD_BC:]
    dtb = jnp.pad(dt_bias, (0, pad)).reshape(1, LANES)
    alog = jnp.pad(a_log, (0, pad)).reshape(1, LANES)
    dsk = jnp.repeat(d_skip, SSD_HEAD_DIM).reshape(1, W)
    head_of_channel = jnp.arange(W, dtype=jnp.int32) // SSD_HEAD_DIM
    expand = (jnp.arange(LANES, dtype=jnp.int32)[:, None] == head_of_channel[None, :]).astype(BF16)

    def rows(b, c):
        return b * nc + c

    def const(shape):
        return pl.BlockSpec(shape, lambda b, c: (0, 0))

    zcol = 3 * SB_WIDTH // W
    bcol = (3 * SB_WIDTH + 2 * W) // SSD_BC
    est = 2 * (L * AB_MAIN * 2 + L * LANES * 4) + 2 * LANES * W * 2 + N_STATE_BYTES + 40 * L * W * 4
    return pl.pallas_call(
        _ssd_body,
        grid=(bsz, nc),
        in_specs=[pl.BlockSpec((L, W), lambda b, c: (rows(b, c), zcol)),
                  pl.BlockSpec((L, W), lambda b, c: (rows(b, c), zcol + 1)),
                  pl.BlockSpec((L, SSD_BC), lambda b, c: (rows(b, c), bcol)),
                  pl.BlockSpec((L, SSD_BC), lambda b, c: (rows(b, c), bcol + 1)),
                  pl.BlockSpec((L, LANES), lambda b, c: (rows(b, c), 0)),
                  const((SSD_CONV, W)), const((SSD_CONV, SSD_BC)), const((SSD_CONV, SSD_BC)),
                  const((1, W)), const((1, SSD_BC)), const((1, SSD_BC)),
                  const((1, LANES)), const((1, LANES)), const((1, W)), const((1, W)),
                  const((LANES, W))],
        out_specs=pl.BlockSpec((L, W), lambda b, c: (rows(b, c), 0)),
        out_shape=jax.ShapeDtypeStruct((bsz * seq, W), BF16),
        scratch_shapes=[pltpu.VMEM((HALO, W), F32), pltpu.VMEM((HALO, SSD_BC), F32),
                        pltpu.VMEM((HALO, SSD_BC), F32), pltpu.VMEM((SSD_STATE, W), F32)],
        compiler_params=_params(("arbitrary", "arbitrary"), est),
        name="ssd_mixer",
    )(proj, proj, proj, proj, dt_raw, cwx, cwb, cwc, cbx, cbb, cbc, dtb, alog, dsk,
      norm_g.reshape(1, W), expand)


N_STATE_BYTES = SSD_STATE * SSD_WIDTH * 4


def _lru_body(g_ref, x_ref, cw_ref, cb_ref, wa_ref, ba_ref, wi_ref, bi_ref, lam_ref, o_ref, tail_ref, h_ref):
    tt = LRU_TT

    @pl.when(pl.program_id(1) == 0)
    def _():
        tail_ref[...] = jnp.zeros_like(tail_ref)
        h_ref[...] = jnp.zeros_like(h_ref)

    xc = _causal_conv(x_ref[...].astype(F32), tail_ref, cw_ref[...], cb_ref[...], LRU_CONV)
    xc_b = xc.astype(BF16)
    pre_a, pre_i = [], []
    for blk in range(LRU_BLOCKS):
        xb = xc_b[:, blk * LRU_BLOCK:(blk + 1) * LRU_BLOCK]
        pre_a.append(_dot(xb, wa_ref[blk]))
        pre_i.append(_dot(xb, wi_ref[blk]))
    r = _sigmoid(jnp.concatenate(pre_a, axis=1) + ba_ref[...])
    gate_i = _sigmoid(jnp.concatenate(pre_i, axis=1) + bi_ref[...])
    log_a = (-LRU_C * r) * _softplus(-lam_ref[...])
    a = jnp.exp(log_a)
    m = -jnp.tanh(log_a) * (a * a + 1.0)
    u = jnp.where(m > 0.0, m * lax.rsqrt(m), 0.0) * (gate_i * xc)

    d = 1
    while d < SUBLANES:
        u = u + a * _shift_in_group(u, d, 0.0)
        a = a * _shift_in_group(a, d, 1.0)
        d *= 2
    carry = h_ref[...]
    groups = []
    for g in range(tt // SUBLANES):
        rows = slice(g * SUBLANES, (g + 1) * SUBLANES)
        groups.append(u[rows, :] + a[rows, :] * carry)
        carry = _last_row_bcast(groups[-1])
    h_ref[...] = carry
    h = jnp.concatenate(groups, axis=0)
    o_ref[...] = (h * _gelu_tanh(g_ref[...].astype(F32))).astype(o_ref.dtype)


def _lru_mixer(proj, conv_w, conv_b, w_a, b_a, w_i, b_i, lam, bsz, seq):
    W = LRU_WIDTH
    nt = seq // LRU_TT

    def const2(shape):
        return pl.BlockSpec(shape, lambda b, t: (0, 0))

    def const3(shape):
        return pl.BlockSpec(shape, lambda b, t: (0, 0, 0))

    est = 4 * LRU_TT * W * 2 + 4 * LRU_BLOCKS * LRU_BLOCK * LRU_BLOCK * 2 + 30 * LRU_TT * W * 4
    return pl.pallas_call(
        _lru_body,
        grid=(bsz, nt),
        in_specs=[pl.BlockSpec((LRU_TT, W), lambda b, t: (b * nt + t, 0)),
                  pl.BlockSpec((LRU_TT, W), lambda b, t: (b * nt + t, 1)),
                  const2((LRU_CONV, W)), const2((1, W)),
                  const3((LRU_BLOCKS, LRU_BLOCK, LRU_BLOCK)), const2((1, W)),
                  const3((LRU_BLOCKS, LRU_BLOCK, LRU_BLOCK)), const2((1, W)),
                  const2((1, W))],
        out_specs=pl.BlockSpec((LRU_TT, W), lambda b, t: (b * nt + t, 0)),
        out_shape=jax.ShapeDtypeStruct((bsz * seq, W), BF16),
        scratch_shapes=[pltpu.VMEM((HALO, W), F32), pltpu.VMEM((SUBLANES, W), F32)],
        compiler_params=_params(("arbitrary", "arbitrary"), est),
        name="rglru_mixer",
    )(proj, proj, conv_w, conv_b.reshape(1, W), w_a.astype(BF16), b_a.reshape(1, W),
      w_i.astype(BF16), b_i.reshape(1, W), lam.reshape(1, W))


def _s5_body(u_ref, wb_ref, wc_ref, pwr_ref, pwi_ref, dsk_ref, wg_ref, bg_ref, o_ref, hr_ref, hi_ref):
    tt = S5_CHUNK
    SW = S5_STATE_WIDTH

    @pl.when(pl.program_id(1) == 0)
    def _():
        hr_ref[...] = jnp.zeros_like(hr_ref)
        hi_ref[...] = jnp.zeros_like(hi_ref)

    u_b = u_ref[...]
    bu = _dot(u_b, wb_ref[...])
    sr, si = bu[:, :SW], bu[:, SW:]
    pwr = pwr_ref[...]
    pwi = pwi_ref[...]
    d = 1
    while d < SUBLANES:
        ar, ai = pwr[d - 1:d, :], pwi[d - 1:d, :]
        sr_s, si_s = _shift_in_group(sr, d, 0.0), _shift_in_group(si, d, 0.0)
        sr, si = sr + (ar * sr_s - ai * si_s), si + (ar * si_s + ai * sr_s)
        d *= 2
    cr, ci = hr_ref[...], hi_ref[...]
    groups_r, groups_i = [], []
    for g in range(tt // SUBLANES):
        rows = slice(g * SUBLANES, (g + 1) * SUBLANES)
        groups_r.append(sr[rows, :] + (pwr * cr - pwi * ci))
        groups_i.append(si[rows, :] + (pwr * ci + pwi * cr))
        cr, ci = _last_row_bcast(groups_r[-1]), _last_row_bcast(groups_i[-1])
    hr_ref[...] = cr
    hi_ref[...] = ci
    sr = jnp.concatenate(groups_r, axis=0)
    si = jnp.concatenate(groups_i, axis=0)
    s_cat = jnp.concatenate([sr, si], axis=1).astype(BF16)
    y = _gelu_tanh(_dot(s_cat, wc_ref[...]) + dsk_ref[...] * u_b.astype(F32))
    o_ref[...] = (y * _sigmoid(_dot(y.astype(BF16), wg_ref[...]) + bg_ref[...])).astype(o_ref.dtype)


def _s5_tables(log_dt, a_re, a_im, b_re, b_im, c_re, c_im):
    G, GC, N = S5_GROUPS, S5_GROUP_CH, S5_STATE
    dt = jnp.exp(log_dt)[:, None]
    lr = jnp.minimum(a_re, -1e-4)
    li = a_im
    mag = jnp.exp(dt * lr)
    ab_i = mag * jnp.sin(dt * li)
    am1_r = jnp.expm1(dt * lr) * jnp.cos(dt * li) - 2.0 * jnp.square(jnp.sin(0.5 * dt * li))
    den = lr * lr + li * li
    f_r = (am1_r * lr + ab_i * li) / den
    f_i = (ab_i * lr - am1_r * li) / den
    bb_r = f_r[..., None] * b_re - f_i[..., None] * b_im
    bb_i = f_r[..., None] * b_im + f_i[..., None] * b_re
    eye = jnp.eye(G, dtype=F32)
    wb_r = jnp.einsum('gnc,gh->gchn', bb_r, eye).reshape(G * GC, G * N)
    wb_i = jnp.einsum('gnc,gh->gchn', bb_i, eye).reshape(G * GC, G * N)
    wc_r = jnp.einsum('gcn,gh->gnhc', c_re, eye).reshape(G * N, G * GC)
    wc_i = jnp.einsum('gcn,gh->gnhc', c_im, eye).reshape(G * N, G * GC)
    wb = jnp.concatenate([wb_r, wb_i], axis=1).astype(BF16)
    wc = jnp.concatenate([wc_r, -wc_i], axis=0).astype(BF16)
    powers = jnp.arange(1, SUBLANES + 1, dtype=F32)[:, None, None] * dt[None]
    pw_mag = jnp.exp(powers * lr)
    pw_r = (pw_mag * jnp.cos(powers * li)).reshape(SUBLANES, G * N)
    pw_i = (pw_mag * jnp.sin(powers * li)).reshape(SUBLANES, G * N)
    return wb, wc, pw_r, pw_i


def _s5_mixer(proj, log_dt, a_re, a_im, b_re, b_im, c_re, c_im, d_skip, w_glu, b_glu, bsz, seq):
    W = S5_WIDTH
    SW = S5_STATE_WIDTH
    tt = S5_CHUNK
    nt = seq // tt
    wb, wc, pw_r, pw_i = _s5_tables(log_dt, a_re, a_im, b_re, b_im, c_re, c_im)

    def const(shape):
        return pl.BlockSpec(shape, lambda b, t: (0, 0))

    ucol = 2 * LRU_WIDTH // W
    est = 8 * W * SW * 2 + 4 * tt * SW * 4 + 2 * W * W * 2 + 24 * tt * SW * 4
    return pl.pallas_call(
        _s5_body,
        grid=(bsz, nt),
        in_specs=[pl.BlockSpec((tt, W), lambda b, t: (b * nt + t, ucol)),
                  const((W, 2 * SW)), const((2 * SW, W)), const((SUBLANES, SW)), const((SUBLANES, SW)),
                  const((1, W)), const((W, W)), const((1, W))],
        out_specs=pl.BlockSpec((tt, W), lambda b, t: (b * nt + t, 0)),
        out_shape=jax.ShapeDtypeStruct((bsz * seq, W), BF16),
        scratch_shapes=[pltpu.VMEM((SUBLANES, SW), F32), pltpu.VMEM((SUBLANES, SW), F32)],
        compiler_params=_params(("arbitrary", "arbitrary"), est),
        name="s5_mixer",
    )(proj, wb, wc, pw_r, pw_i, d_skip.reshape(1, W), w_glu.astype(BF16), b_glu.reshape(1, W))


def _out_proj_body(h_ref, a_ref, b_ref, wa_ref, wb_ref, o_ref):
    a = a_ref[...]
    b = b_ref[...]
    for c in range(o_ref.shape[1] // OUT_TN):
        cols = slice(c * OUT_TN, (c + 1) * OUT_TN)
        o_ref[:, cols] = h_ref[:, cols] + _dot(a, wa_ref[:, cols]) + _dot(b, wb_ref[:, cols])


def _out_proj(h, a, b, w):
    t, d = h.shape
    ka, kb = a.shape[1], b.shape[1]
    wa, wb = w[:ka], w[ka:]
    resident = dict(pipeline_mode=pl.Buffered(1))
    est = (ka + kb) * d * 2 + 4 * OUT_TM * d * 4 + 2 * OUT_TM * (ka + kb) * 2 + 2 * OUT_TM * OUT_TN * 4
    return pl.pallas_call(
        _out_proj_body,
        grid=(t // OUT_TM,),
        in_specs=[pl.BlockSpec((OUT_TM, d), lambda i: (i, 0)),
                  pl.BlockSpec((OUT_TM, ka), lambda i: (i, 0)),
                  pl.BlockSpec((OUT_TM, kb), lambda i: (i, 0)),
                  pl.BlockSpec((ka, d), lambda i: (0, 0), **resident),
                  pl.BlockSpec((kb, d), lambda i: (0, 0), **resident)],
        out_specs=pl.BlockSpec((OUT_TM, d), lambda i: (i, 0)),
        out_shape=jax.ShapeDtypeStruct((t, d), F32),
        compiler_params=_params(("arbitrary",), est),
        name="out_proj",
    )(h, a, b, wa, wb)


def _ffn_body(*refs, tiles_per_seq, final_norm):
    if final_norm:
        h_ref, g_ref, wv_ref, wg_ref, cw_ref, cb_ref, wd_ref, gf_ref, o_ref, yn_ref, tail_ref = refs
    else:
        h_ref, g_ref, wv_ref, wg_ref, cw_ref, cb_ref, wd_ref, o_ref, yn_ref, tail_ref = refs
    i = pl.program_id(0)
    j = pl.program_id(1)
    tm = FFN_TM

    @pl.when(j == 0)
    def _():
        x = h_ref[...]
        yn_ref[...] = (_rms_scale(x) * g_ref[...]).astype(BF16)
        o_ref[...] = x

    yn = yn_ref[...]
    first = i % tiles_per_seq == 0
    acts = []
    for c in range(FFN_TF // FFN_SUB):
        cols = slice(c * FFN_SUB, (c + 1) * FFN_SUB)
        val = _dot(yn, wv_ref[:, cols])
        gate = _dot(yn, wg_ref[:, cols])
        tail = jnp.where(first, 0.0, tail_ref[j, :, cols])
        tail_ref[j, :, cols] = gate[tm - HALO:, :]
        ext = jnp.concatenate([tail, gate], axis=0)
        w = cw_ref[:, cols]
        cv = cb_ref[:, cols] + w[2:3, :] * gate
        for k in range(FFN_CONV - 1):
            off = HALO - (FFN_CONV - 1) + k
            cv = cv + w[k:k + 1, :] * ext[off:off + tm, :]
        acts.append((_silu(cv) * val).astype(BF16))
    o_ref[...] += _dot(jnp.concatenate(acts, axis=1), wd_ref[...])

    if final_norm:
        @pl.when(j == pl.num_programs(1) - 1)
        def _():
            o_ref[...] = _rms_scale(o_ref[...]) * gf_ref[...]


def _conv_ffn(h, g, w_up, conv_w, conv_b, w_down, seq, g_final=None):
    t, d = h.shape
    nj = D_FF // FFN_TF
    final_norm = g_final is not None
    in_specs = [pl.BlockSpec((FFN_TM, d), lambda i, j: (i, 0)),
                pl.BlockSpec((1, d), lambda i, j: (0, 0)),
                pl.BlockSpec((d, FFN_TF), lambda i, j: (0, j)),
                pl.BlockSpec((d, FFN_TF), lambda i, j: (0, nj + j)),
                pl.BlockSpec((FFN_CONV, FFN_TF), lambda i, j: (0, j)),
                pl.BlockSpec((1, FFN_TF), lambda i, j: (0, j)),
                pl.BlockSpec((FFN_TF, d), lambda i, j: (j, 0))]
    args = [h, g.reshape(1, d), w_up, w_up, conv_w, conv_b.reshape(1, D_FF), w_down]
    if final_norm:
        in_specs.append(pl.BlockSpec((1, d), lambda i, j: (0, 0)))
        args.append(g_final.reshape(1, d))
    est = (4 * FFN_TM * d * 4 + FFN_TM * d * 2 + 4 * d * FFN_TF * 2 + 2 * FFN_TF * d * 2
           + FFN_TM * FFN_TF * 2 + 6 * FFN_TM * FFN_SUB * 4)
    return pl.pallas_call(
        functools.partial(_ffn_body, tiles_per_seq=seq // FFN_TM, final_norm=final_norm),
        grid=(t // FFN_TM, nj),
        in_specs=in_specs,
        out_specs=pl.BlockSpec((FFN_TM, d), lambda i, j: (i, 0)),
        out_shape=jax.ShapeDtypeStruct((t, d), F32),
        scratch_shapes=[pltpu.VMEM((FFN_TM, d), BF16), pltpu.VMEM((nj, HALO, FFN_TF), F32)],
        compiler_params=_params(("arbitrary", "arbitrary"), est),
        name="conv_ffn_final" if final_norm else "conv_ffn",
    )(*args)


def _attn_ssd_layer(h, g, w_in, conv_w, conv_b, dt_bias, a_log, d_skip, ssd_norm, w_out, bsz, seq):
    w_main = w_in[:, :AB_MAIN].astype(BF16)
    w_dt = jnp.pad(w_in[:, AB_MAIN:], ((0, 0), (0, LANES - SSD_HEADS))).astype(BF16)
    proj, dt_raw = _norm_proj(h, g, w_main, w_dt)
    o_a = _sb_attention(proj, bsz, seq)
    o_b = _ssd_mixer(proj, dt_raw, conv_w, conv_b, dt_bias, a_log, d_skip, ssd_norm, bsz, seq)
    return _out_proj(h, o_a, o_b, w_out.astype(BF16))


def _lru_s5_layer(h, g, w_in, conv_w, conv_b, w_a, b_a, w_i, b_i, lam, log_dt, a_re, a_im, b_re, b_im,
                  c_re, c_im, d_skip, w_glu, b_glu, w_out, bsz, seq):
    proj = _norm_proj(h, g, w_in.astype(BF16))
    o_c = _lru_mixer(proj, conv_w, conv_b, w_a, b_a, w_i, b_i, lam, bsz, seq)
    o_d = _s5_mixer(proj, log_dt, a_re, a_im, b_re, b_im, c_re, c_im, d_skip, w_glu, b_glu, bsz, seq)
    return _out_proj(h, o_c, o_d, w_out.astype(BF16))


def kernel(x, norm_mix, norm_ffn, norm_final, ab_w_in, ab_conv_w, ab_conv_b, ab_dt_bias, ab_a_log, ab_d_skip, ab_ssd_norm, ab_w_out, cd_w_in, cd_conv_w, cd_conv_b, cd_w_a, cd_b_a, cd_w_i, cd_b_i, cd_lambda, cd_log_dt, cd_a_re, cd_a_im, cd_b_re, cd_b_im, cd_c_re, cd_c_im, cd_d_skip, cd_w_glu, cd_b_glu, cd_w_out, ffn_w_up, ffn_conv_w, ffn_conv_b, ffn_w_down):
    bsz, seq, d = x.shape
    depth = norm_mix.shape[0]
    h = x.reshape(bsz * seq, d)
    for layer in range(depth):
        j = layer // 2
        if layer % 2 == 0:
            h = _attn_ssd_layer(h, norm_mix[layer], ab_w_in[j], ab_conv_w[j], ab_conv_b[j], ab_dt_bias[j],
                                ab_a_log[j], ab_d_skip[j], ab_ssd_norm[j], ab_w_out[j], bsz, seq)
        else:
            h = _lru_s5_layer(h, norm_mix[layer], cd_w_in[j], cd_conv_w[j], cd_conv_b[j], cd_w_a[j], cd_b_a[j],
                              cd_w_i[j], cd_b_i[j], cd_lambda[j], cd_log_dt[j], cd_a_re[j], cd_a_im[j],
                              cd_b_re[j], cd_b_im[j], cd_c_re[j], cd_c_im[j], cd_d_skip[j], cd_w_glu[j],
                              cd_b_glu[j], cd_w_out[j], bsz, seq)
        h = _conv_ffn(h, norm_ffn[layer], ffn_w_up[layer].astype(BF16), ffn_conv_w[layer], ffn_conv_b[layer],
                      ffn_w_down[layer].astype(BF16), seq, g_final=norm_final if layer == depth - 1 else None)
    return h.reshape(bsz, seq, d)
```

```python
import functools

import jax
import jax.numpy as jnp
from jax import lax
from jax.experimental import pallas as pl
from jax.experimental.pallas import tpu as pltpu

F32 = jnp.float32
BF16 = jnp.bfloat16

D_MODEL = 2048
SB_HEADS = 4
SB_HEAD_DIM = 128
SB_WIDTH = SB_HEADS * SB_HEAD_DIM
SSD_HEADS = 24
SSD_HEAD_DIM = 64
SSD_WIDTH = SSD_HEADS * SSD_HEAD_DIM
SSD_GROUPS = 4
SSD_HEADS_PER_GROUP = SSD_HEADS // SSD_GROUPS
SSD_STATE = 128
SSD_CONV = 4
SSD_CHUNK = 128
SSD_BC = SSD_GROUPS * SSD_STATE
AB_MAIN = 3 * SB_WIDTH + 2 * SSD_WIDTH + 2 * SSD_BC
LRU_WIDTH = 1536
LRU_BLOCK = 128
LRU_BLOCKS = LRU_WIDTH // LRU_BLOCK
LRU_CONV = 4
LRU_C = 8.0
S5_GROUPS = 32
S5_GROUP_CH = 16
S5_WIDTH = S5_GROUPS * S5_GROUP_CH
S5_STATE = 64
S5_CHUNK = 128
S5_STATE_WIDTH = S5_GROUPS * S5_STATE
CD_IN = 2 * LRU_WIDTH + S5_WIDTH
D_FF = 4096
FFN_CONV = 3
EPS = 1e-6

LANES = 128
SUBLANES = 8
VMEM_CAP_BYTES = 60000 * 1024

PROJ_TM = 512
PROJ_TN = 512
OUT_TM = 512
OUT_TN = 512
FFN_TM = 512
FFN_TF = 1024
FFN_SUB = 256
SB_TQ = 256
SB_DEAD_LOG2 = 160.0
LRU_TT = 256
S5_TT = 256
SSD_TILE = 256
HALO = SUBLANES


def _vmem_limit(est_bytes):
    return int(min(max(est_bytes * 5 // 4, 32 * 1024 * 1024), VMEM_CAP_BYTES))


def _params(semantics, est_bytes):
    return pltpu.CompilerParams(dimension_semantics=semantics, vmem_limit_bytes=_vmem_limit(est_bytes))


def _dot(a, b):
    return jnp.dot(a, b, preferred_element_type=F32)


def _dot_nt(a, b):
    return lax.dot_general(a, b, (((1,), (1,)), ((), ())), preferred_element_type=F32)


def _dot_tn(a, b):
    return lax.dot_general(a, b, (((0,), (0,)), ((), ())), preferred_element_type=F32)


def _split_dot(x, m, terms, x_on_left=True):
    acc = None
    r = x
    for _ in range(terms):
        p = r.astype(BF16)
        d = _dot(p, m) if x_on_left else _dot(m, p)
        acc = d if acc is None else acc + d
        r = r - p.astype(F32)
    return acc


def _softplus(x):
    return jnp.maximum(x, 0.0) + jnp.log(1.0 + jnp.exp(-jnp.abs(x)))


def _sigmoid(x):
    return 0.5 + 0.5 * jnp.tanh(0.5 * x)


def _silu(x):
    hx = 0.5 * x
    return hx + hx * jnp.tanh(hx)


def _gelu_tanh(x):
    hx = 0.5 * x
    return hx + hx * jnp.tanh(x * (0.7978845608028654 + (0.7978845608028654 * 0.044715) * (x * x)))


def _rms_scale(x):
    return x * lax.rsqrt(jnp.mean(x * x, axis=-1, keepdims=True) + EPS)


def _shift_in_group(x, d, fill):
    rows, width = x.shape
    x3 = x.reshape(rows // SUBLANES, SUBLANES, width)
    sub = lax.broadcasted_iota(jnp.int32, x3.shape, 1)
    return jnp.where(sub >= d, pltpu.roll(x3, d, axis=1), fill).reshape(rows, width)


def _last_row_bcast(x):
    return jnp.broadcast_to(x[SUBLANES - 1:SUBLANES, :], x.shape)


def _rows_back(cur, tail, shift):
    rows, width = cur.shape
    groups = rows // SUBLANES
    rolled = pltpu.roll(cur.reshape(groups, SUBLANES, width), shift, axis=1)
    rolled_tail = pltpu.roll(tail.reshape(1, SUBLANES, width), shift, axis=1)
    prev = jnp.concatenate([rolled_tail, rolled[:groups - 1]], axis=0)
    sub = lax.broadcasted_iota(jnp.int32, rolled.shape, 1)
    return jnp.where(sub >= shift, rolled, prev).reshape(rows, width)


def _causal_conv(cur, tail, w, bias, taps):
    y = bias + w[taps - 1:taps, :] * cur
    for k in range(taps - 1):
        y = y + w[k:k + 1, :] * _rows_back(cur, tail, taps - 1 - k)
    return y


def _causal_conv_carried(cur, tail_ref, w, bias, taps):
    tail = tail_ref[...]
    tail_ref[...] = cur[cur.shape[0] - HALO:, :]
    return _causal_conv(cur, tail, w, bias, taps)


def _norm_proj_body(*refs, has_aux):
    if has_aux:
        h_ref, g_ref, w_ref, wa_ref, o_ref, oa_ref = refs
    else:
        h_ref, g_ref, w_ref, o_ref = refs
    yn = (_rms_scale(h_ref[...]) * g_ref[...]).astype(BF16)
    for c in range(w_ref.shape[1] // PROJ_TN):
        cols = slice(c * PROJ_TN, (c + 1) * PROJ_TN)
        o_ref[:, cols] = _dot(yn, w_ref[:, cols]).astype(o_ref.dtype)
    if has_aux:
        oa_ref[...] = _dot(yn, wa_ref[...])


def _norm_proj(h, g, w, w_aux=None):
    t, d = h.shape
    n = w.shape[1]
    has_aux = w_aux is not None
    resident = dict(pipeline_mode=pl.Buffered(1))
    in_specs = [pl.BlockSpec((PROJ_TM, d), lambda i: (i, 0)),
                pl.BlockSpec((1, d), lambda i: (0, 0)),
                pl.BlockSpec((d, n), lambda i: (0, 0), **resident)]
    out_shape = [jax.ShapeDtypeStruct((t, n), BF16)]
    out_specs = [pl.BlockSpec((PROJ_TM, n), lambda i: (i, 0))]
    args = [h, g.reshape(1, d), w]
    if has_aux:
        in_specs.append(pl.BlockSpec((d, LANES), lambda i: (0, 0), **resident))
        out_shape.append(jax.ShapeDtypeStruct((t, LANES), F32))
        out_specs.append(pl.BlockSpec((PROJ_TM, LANES), lambda i: (i, 0)))
        args.append(w_aux)
    est = (d * (n + LANES) * 2 + 2 * PROJ_TM * d * 4 + 2 * PROJ_TM * (n * 2 + LANES * 4)
           + PROJ_TM * d * 2 + 2 * PROJ_TM * PROJ_TN * 4)
    out = pl.pallas_call(
        functools.partial(_norm_proj_body, has_aux=has_aux),
        grid=(t // PROJ_TM,),
        in_specs=in_specs, out_specs=out_specs, out_shape=out_shape,
        compiler_params=_params(("arbitrary",), est),
        name="norm_proj_aux" if has_aux else "norm_proj",
    )(*args)
    return out if has_aux else out[0]


def _sb_body(q_ref, k_ref, v_ref, o_ref):
    i = pl.program_id(1)
    tq = SB_TQ
    dh = SB_HEAD_DIM
    row = lax.broadcasted_iota(jnp.int32, (tq, tq), 0)
    col = lax.broadcasted_iota(jnp.int32, (tq, tq), 1)
    tri = (row >= col).astype(BF16)
    strict = col < row
    qscale = (dh ** -0.5) * 1.4426950408889634
    qs = [(q_ref[:, h * dh:(h + 1) * dh].astype(F32) * qscale).astype(BF16) for h in range(SB_HEADS)]

    def block(h, start, masked, nls, acc):
        k = k_ref[pl.ds(start, tq), h * dh:(h + 1) * dh]
        v = v_ref[pl.ds(start, tq), h * dh:(h + 1) * dh]
        zb = _dot_nt(qs[h], k)
        nl = jnp.maximum(zb, 0.0) + jnp.log2(1.0 + jnp.exp2(-jnp.abs(zb)))
        if masked:
            nl = jnp.where(strict, nl, 0.0)
        suffix = _dot(nl.astype(BF16), tri)
        w = jnp.exp2(zb - suffix - nls)
        if masked:
            w = jnp.where(strict, w, 0.0)
        acc = acc + _dot(w.astype(BF16), v)
        nls = nls + jnp.sum(nl, axis=-1, keepdims=True)
        return nls, acc

    def step(kb, masked, carry):
        start = pl.multiple_of(kb * tq, tq)
        return tuple(block(h, start, masked, *carry[h]) for h in range(SB_HEADS))

    def live(carry):
        least = functools.reduce(jnp.minimum, [c[0] for c in carry])
        return jnp.min(least) < SB_DEAD_LOG2

    def body(state):
        n, _, carry = state
        carry = step(i - 1 - n, False, carry)
        return n + 1, live(carry), carry

    init = tuple((jnp.zeros((tq, 1), F32), jnp.zeros((tq, dh), F32)) for _ in range(SB_HEADS))
    carry = step(i, True, init)
    _, _, carry = lax.while_loop(lambda s: jnp.logical_and(s[0] < i, s[1]), body,
                                 (jnp.int32(0), live(carry), carry))
    o_ref[...] = jnp.concatenate([c[1] for c in carry], axis=1).astype(o_ref.dtype)


def _sb_attention(proj, bsz, seq):
    nq = seq // SB_TQ
    est = 2 * seq * SB_WIDTH * 2 + 64 * SB_TQ * SB_TQ * 4
    resident = dict(pipeline_mode=pl.Buffered(1))
    return pl.pallas_call(
        _sb_body,
        grid=(bsz, nq),
        in_specs=[pl.BlockSpec((SB_TQ, SB_WIDTH), lambda b, i: (b * nq + i, 0)),
                  pl.BlockSpec((seq, SB_WIDTH), lambda b, i: (b, 1), **resident),
                  pl.BlockSpec((seq, SB_WIDTH), lambda b, i: (b, 2), **resident)],
        out_specs=pl.BlockSpec((SB_TQ, SB_WIDTH), lambda b, i: (b * nq + i, 0)),
        out_shape=jax.ShapeDtypeStruct((bsz * seq, SB_WIDTH), BF16),
        compiler_params=_params(("arbitrary", "arbitrary"), est),
        name="sb_attention",
    )(proj, proj, proj)


def _ssd_body(z_ref, x_ref, b_ref, c_ref, dt_ref, cwx_ref, cwb_ref, cwc_ref, cbx_ref, cbb_ref, cbc_ref,
              dtb_ref, alog_ref, dsk_ref, ng_ref, e_ref, o_ref, tx_ref, tb_ref, tc_ref, st_ref):
    L = SSD_CHUNK

    @pl.when(pl.program_id(1) == 0)
    def _():
        tx_ref[...] = jnp.zeros_like(tx_ref)
        tb_ref[...] = jnp.zeros_like(tb_ref)
        tc_ref[...] = jnp.zeros_like(tc_ref)
        st_ref[...] = jnp.zeros_like(st_ref)

    xs_t = _silu(_causal_conv_carried(x_ref[...].astype(F32), tx_ref, cwx_ref[...], cbx_ref[...], SSD_CONV))
    bm_t = _silu(_causal_conv_carried(b_ref[...].astype(F32), tb_ref, cwb_ref[...], cbb_ref[...], SSD_CONV))
    cm_t = _silu(_causal_conv_carried(c_ref[...].astype(F32), tc_ref, cwc_ref[...], cbc_ref[...], SSD_CONV))
    dt_t = _softplus(dt_ref[...] + dtb_ref[...])
    da_t = dt_t * (-jnp.exp(alog_ref[...]))
    row = lax.broadcasted_iota(jnp.int32, (L, L), 0)
    col = lax.broadcasted_iota(jnp.int32, (L, L), 1)
    causal = row >= col
    tri = causal.astype(BF16)
    expand = e_ref[...]
    low_half = lax.broadcasted_iota(jnp.int32, (L, LANES), 1) < SSD_HEAD_DIM
    state = st_ref[...]
    ys = []
    for ch in range(SSD_TILE // L):
        rows = slice(ch * L, (ch + 1) * L)
        state, y_ch = _ssd_chunk(xs_t[rows, :], bm_t[rows, :], cm_t[rows, :], dt_t[rows, :], da_t[rows, :],
                                 state, causal, tri, expand, low_half, dsk_ref[...])
        ys.append(y_ch)
    st_ref[...] = state
    y = jnp.concatenate(ys, axis=0) * _silu(z_ref[...].astype(F32))
    o_ref[...] = (_rms_scale(y) * ng_ref[...]).astype(o_ref.dtype)


def _ssd_chunk(xs, bm, cm, dt, da, prev, causal, tri, expand, low_half, dsk):
    L = SSD_CHUNK
    N = SSD_STATE
    GW = SSD_HEADS_PER_GROUP * SSD_HEAD_DIM
    cs = _split_dot(da, tri, 3, x_on_left=False)
    cs_t = cs.T
    cs_last = cs[L - 1:L, :]
    dt_e = _split_dot(dt, expand, 2)
    ecs_e = _split_dot(jnp.exp(cs), expand, 2)
    toend_e = _split_dot(jnp.exp(cs_last - cs), expand, 2)

    xdt = xs * dt_e
    bm_b = bm.astype(BF16)
    cm_b = cm.astype(BF16)
    xte_b = (xdt * toend_e).astype(BF16)
    prev_b = prev.astype(BF16)

    y_diag, y_off, new_state = [], [], []
    for g in range(SSD_GROUPS):
        cg = cm_b[:, g * N:(g + 1) * N]
        bg = bm_b[:, g * N:(g + 1) * N]
        cb = _dot_nt(cg, bg)
        for pair in range(SSD_HEADS_PER_GROUP // 2):
            h0 = g * SSD_HEADS_PER_GROUP + 2 * pair
            mats = []
            for h in (h0, h0 + 1):
                diff = cs[:, h:h + 1] - cs_t[h:h + 1, :]
                decay = jnp.exp(jnp.where(causal, diff, -jnp.inf))
                mats.append((cb * decay).astype(BF16))
            xp = xdt[:, h0 * SSD_HEAD_DIM:(h0 + 2) * SSD_HEAD_DIM]
            rhs = jnp.concatenate([jnp.where(low_half, xp, 0.0), jnp.where(low_half, 0.0, xp)], axis=0)
            y_diag.append(_dot(jnp.concatenate(mats, axis=1), rhs.astype(BF16)))
        y_off.append(_dot(cg, prev_b[:, g * GW:(g + 1) * GW]))
        new_state.append(_dot_tn(bg, xte_b[:, g * GW:(g + 1) * GW]))

    state = prev * ecs_e[L - 1:L, :] + jnp.concatenate(new_state, axis=1)
    return state, jnp.concatenate(y_diag, axis=1) + jnp.concatenate(y_off, axis=1) * ecs_e + xs * dsk


def _ssd_mixer(proj, dt_raw, conv_w, conv_b, dt_bias, a_log, d_skip, norm_g, bsz, seq):
    L = SSD_TILE
    nc = seq // L
    W = SSD_WIDTH
    pad = LANES - SSD_HEADS
    cwx, cwb, cwc = conv_w[:, :W], conv_w[:, W:W + SSD_BC], conv_w[:, W + SSD_BC:]
    cb2 = conv_b.reshape(1, -1)
    cbx, cbb, cbc = cb2[:, :W], cb2[:, W:W + SSD_BC], cb2[:, W + SSD_BC:]
    dtb = jnp.pad(dt_bias, (0, pad)).reshape(1, LANES)
    alog = jnp.pad(a_log, (0, pad)).reshape(1, LANES)
    dsk = jnp.repeat(d_skip, SSD_HEAD_DIM).reshape(1, W)
    head_of_channel = jnp.arange(W, dtype=jnp.int32) // SSD_HEAD_DIM
    expand = (jnp.arange(LANES, dtype=jnp.int32)[:, None] == head_of_channel[None, :]).astype(BF16)

    def rows(b, c):
        return b * nc + c

    def const(shape):
        return pl.BlockSpec(shape, lambda b, c: (0, 0))

    zcol = 3 * SB_WIDTH // W
    bcol = (3 * SB_WIDTH + 2 * W) // SSD_BC
    est = 2 * (L * AB_MAIN * 2 + L * LANES * 4) + 2 * LANES * W * 2 + N_STATE_BYTES + 40 * L * W * 4
    return pl.pallas_call(
        _ssd_body,
        grid=(bsz, nc),
        in_specs=[pl.BlockSpec((L, W), lambda b, c: (rows(b, c), zcol)),
                  pl.BlockSpec((L, W), lambda b, c: (rows(b, c), zcol + 1)),
                  pl.BlockSpec((L, SSD_BC), lambda b, c: (rows(b, c), bcol)),
                  pl.BlockSpec((L, SSD_BC), lambda b, c: (rows(b, c), bcol + 1)),
                  pl.BlockSpec((L, LANES), lambda b, c: (rows(b, c), 0)),
                  const((SSD_CONV, W)), const((SSD_CONV, SSD_BC)), const((SSD_CONV, SSD_BC)),
                  const((1, W)), const((1, SSD_BC)), const((1, SSD_BC)),
                  const((1, LANES)), const((1, LANES)), const((1, W)), const((1, W)),
                  const((LANES, W))],
        out_specs=pl.BlockSpec((L, W), lambda b, c: (rows(b, c), 0)),
        out_shape=jax.ShapeDtypeStruct((bsz * seq, W), BF16),
        scratch_shapes=[pltpu.VMEM((HALO, W), F32), pltpu.VMEM((HALO, SSD_BC), F32),
                        pltpu.VMEM((HALO, SSD_BC), F32), pltpu.VMEM((SSD_STATE, W), F32)],
        compiler_params=_params(("arbitrary", "arbitrary"), est),
        name="ssd_mixer",
    )(proj, proj, proj, proj, dt_raw, cwx, cwb, cwc, cbx, cbb, cbc, dtb, alog, dsk,
      norm_g.reshape(1, W), expand)


N_STATE_BYTES = SSD_STATE * SSD_WIDTH * 4


def _lru_body(g_ref, x_ref, cw_ref, cb_ref, wa_ref, ba_ref, wi_ref, bi_ref, lam_ref, o_ref, tail_ref, h_ref):
    tt = LRU_TT

    @pl.when(pl.program_id(1) == 0)
    def _():
        tail_ref[...] = jnp.zeros_like(tail_ref)
        h_ref[...] = jnp.zeros_like(h_ref)

    xc = _causal_conv_carried(x_ref[...].astype(F32), tail_ref, cw_ref[...], cb_ref[...], LRU_CONV)
    xc_b = xc.astype(BF16)
    pre_a, pre_i = [], []
    for blk in range(LRU_BLOCKS):
        xb = xc_b[:, blk * LRU_BLOCK:(blk + 1) * LRU_BLOCK]
        pre_a.append(_dot(xb, wa_ref[blk]))
        pre_i.append(_dot(xb, wi_ref[blk]))
    r = _sigmoid(jnp.concatenate(pre_a, axis=1) + ba_ref[...])
    gate_i = _sigmoid(jnp.concatenate(pre_i, axis=1) + bi_ref[...])
    log_a = (-LRU_C * r) * _softplus(-lam_ref[...])
    a = jnp.exp(log_a)
    m = -jnp.tanh(log_a) * (a * a + 1.0)
    u = jnp.where(m > 0.0, m * lax.rsqrt(m), 0.0) * (gate_i * xc)

    d = 1
    while d < SUBLANES:
        u = u + a * _shift_in_group(u, d, 0.0)
        a = a * _shift_in_group(a, d, 1.0)
        d *= 2
    carry = h_ref[...]
    groups = []
    for g in range(tt // SUBLANES):
        rows = slice(g * SUBLANES, (g + 1) * SUBLANES)
        groups.append(u[rows, :] + a[rows, :] * carry)
        carry = _last_row_bcast(groups[-1])
    h_ref[...] = carry
    h = jnp.concatenate(groups, axis=0)
    o_ref[...] = (h * _gelu_tanh(g_ref[...].astype(F32))).astype(o_ref.dtype)


def _lru_mixer(proj, conv_w, conv_b, w_a, b_a, w_i, b_i, lam, bsz, seq):
    W = LRU_WIDTH
    nt = seq // LRU_TT

    def const2(shape):
        return pl.BlockSpec(shape, lambda b, t: (0, 0))

    def const3(shape):
        return pl.BlockSpec(shape, lambda b, t: (0, 0, 0))

    est = 4 * LRU_TT * W * 2 + 4 * LRU_BLOCKS * LRU_BLOCK * LRU_BLOCK * 2 + 30 * LRU_TT * W * 4
    return pl.pallas_call(
        _lru_body,
        grid=(bsz, nt),
        in_specs=[pl.BlockSpec((LRU_TT, W), lambda b, t: (b * nt + t, 0)),
                  pl.BlockSpec((LRU_TT, W), lambda b, t: (b * nt + t, 1)),
                  const2((LRU_CONV, W)), const2((1, W)),
                  const3((LRU_BLOCKS, LRU_BLOCK, LRU_BLOCK)), const2((1, W)),
                  const3((LRU_BLOCKS, LRU_BLOCK, LRU_BLOCK)), const2((1, W)),
                  const2((1, W))],
        out_specs=pl.BlockSpec((LRU_TT, W), lambda b, t: (b * nt + t, 0)),
        out_shape=jax.ShapeDtypeStruct((bsz * seq, W), BF16),
        scratch_shapes=[pltpu.VMEM((HALO, W), F32), pltpu.VMEM((SUBLANES, W), F32)],
        compiler_params=_params(("arbitrary", "arbitrary"), est),
        name="rglru_mixer",
    )(proj, proj, conv_w, conv_b.reshape(1, W), w_a.astype(BF16), b_a.reshape(1, W),
      w_i.astype(BF16), b_i.reshape(1, W), lam.reshape(1, W))


def _s5_body(u_ref, wb_ref, wc_ref, pwr_ref, pwi_ref, dsk_ref, wg_ref, bg_ref, o_ref, hr_ref, hi_ref):
    tt = S5_TT
    SW = S5_STATE_WIDTH

    @pl.when(pl.program_id(1) == 0)
    def _():
        hr_ref[...] = jnp.zeros_like(hr_ref)
        hi_ref[...] = jnp.zeros_like(hi_ref)

    u_b = u_ref[...]
    bu = _dot(u_b, wb_ref[...])
    sr, si = bu[:, :SW], bu[:, SW:]
    pwr = pwr_ref[...]
    pwi = pwi_ref[...]
    d = 1
    while d < SUBLANES:
        ar, ai = pwr[d - 1:d, :], pwi[d - 1:d, :]
        sr_s, si_s = _shift_in_group(sr, d, 0.0), _shift_in_group(si, d, 0.0)
        sr, si = sr + (ar * sr_s - ai * si_s), si + (ar * si_s + ai * sr_s)
        d *= 2
    cr, ci = hr_ref[...], hi_ref[...]
    groups_r, groups_i = [], []
    for g in range(tt // SUBLANES):
        rows = slice(g * SUBLANES, (g + 1) * SUBLANES)
        groups_r.append(sr[rows, :] + (pwr * cr - pwi * ci))
        groups_i.append(si[rows, :] + (pwr * ci + pwi * cr))
        cr, ci = _last_row_bcast(groups_r[-1]), _last_row_bcast(groups_i[-1])
    hr_ref[...] = cr
    hi_ref[...] = ci
    sr = jnp.concatenate(groups_r, axis=0)
    si = jnp.concatenate(groups_i, axis=0)
    s_cat = jnp.concatenate([sr, si], axis=1).astype(BF16)
    y = _gelu_tanh(_dot(s_cat, wc_ref[...]) + dsk_ref[...] * u_b.astype(F32))
    o_ref[...] = (y * _sigmoid(_dot(y.astype(BF16), wg_ref[...]) + bg_ref[...])).astype(o_ref.dtype)


def _s5_tables(log_dt, a_re, a_im, b_re, b_im, c_re, c_im):
    G, GC, N = S5_GROUPS, S5_GROUP_CH, S5_STATE
    dt = jnp.exp(log_dt)[:, None]
    lr = jnp.minimum(a_re, -1e-4)
    li = a_im
    mag = jnp.exp(dt * lr)
    ab_i = mag * jnp.sin(dt * li)
    am1_r = jnp.expm1(dt * lr) * jnp.cos(dt * li) - 2.0 * jnp.square(jnp.sin(0.5 * dt * li))
    den = lr * lr + li * li
    f_r = (am1_r * lr + ab_i * li) / den
    f_i = (ab_i * lr - am1_r * li) / den
    bb_r = f_r[..., None] * b_re - f_i[..., None] * b_im
    bb_i = f_r[..., None] * b_im + f_i[..., None] * b_re
    eye = jnp.eye(G, dtype=F32)
    wb_r = jnp.einsum('gnc,gh->gchn', bb_r, eye).reshape(G * GC, G * N)
    wb_i = jnp.einsum('gnc,gh->gchn', bb_i, eye).reshape(G * GC, G * N)
    wc_r = jnp.einsum('gcn,gh->gnhc', c_re, eye).reshape(G * N, G * GC)
    wc_i = jnp.einsum('gcn,gh->gnhc', c_im, eye).reshape(G * N, G * GC)
    wb = jnp.concatenate([wb_r, wb_i], axis=1).astype(BF16)
    wc = jnp.concatenate([wc_r, -wc_i], axis=0).astype(BF16)
    powers = jnp.arange(1, SUBLANES + 1, dtype=F32)[:, None, None] * dt[None]
    pw_mag = jnp.exp(powers * lr)
    pw_r = (pw_mag * jnp.cos(powers * li)).reshape(SUBLANES, G * N)
    pw_i = (pw_mag * jnp.sin(powers * li)).reshape(SUBLANES, G * N)
    return wb, wc, pw_r, pw_i


def _s5_mixer(proj, log_dt, a_re, a_im, b_re, b_im, c_re, c_im, d_skip, w_glu, b_glu, bsz, seq):
    W = S5_WIDTH
    SW = S5_STATE_WIDTH
    tt = S5_TT
    nt = seq // tt
    wb, wc, pw_r, pw_i = _s5_tables(log_dt, a_re, a_im, b_re, b_im, c_re, c_im)

    def const(shape):
        return pl.BlockSpec(shape, lambda b, t: (0, 0))

    ucol = 2 * LRU_WIDTH // W
    est = 8 * W * SW * 2 + 4 * tt * SW * 4 + 2 * W * W * 2 + 24 * tt * SW * 4
    return pl.pallas_call(
        _s5_body,
        grid=(bsz, nt),
        in_specs=[pl.BlockSpec((tt, W), lambda b, t: (b * nt + t, ucol)),
                  const((W, 2 * SW)), const((2 * SW, W)), const((SUBLANES, SW)), const((SUBLANES, SW)),
                  const((1, W)), const((W, W)), const((1, W))],
        out_specs=pl.BlockSpec((tt, W), lambda b, t: (b * nt + t, 0)),
        out_shape=jax.ShapeDtypeStruct((bsz * seq, W), BF16),
        scratch_shapes=[pltpu.VMEM((SUBLANES, SW), F32), pltpu.VMEM((SUBLANES, SW), F32)],
        compiler_params=_params(("arbitrary", "arbitrary"), est),
        name="s5_mixer",
    )(proj, wb, wc, pw_r, pw_i, d_skip.reshape(1, W), w_glu.astype(BF16), b_glu.reshape(1, W))


def _out_proj_body(h_ref, a_ref, b_ref, wa_ref, wb_ref, o_ref):
    a = a_ref[...]
    b = b_ref[...]
    for c in range(o_ref.shape[1] // OUT_TN):
        cols = slice(c * OUT_TN, (c + 1) * OUT_TN)
        o_ref[:, cols] = h_ref[:, cols] + _dot(a, wa_ref[:, cols]) + _dot(b, wb_ref[:, cols])


def _out_proj(h, a, b, w):
    t, d = h.shape
    ka, kb = a.shape[1], b.shape[1]
    wa, wb = w[:ka], w[ka:]
    resident = dict(pipeline_mode=pl.Buffered(1))
    est = (ka + kb) * d * 2 + 4 * OUT_TM * d * 4 + 2 * OUT_TM * (ka + kb) * 2 + 2 * OUT_TM * OUT_TN * 4
    return pl.pallas_call(
        _out_proj_body,
        grid=(t // OUT_TM,),
        in_specs=[pl.BlockSpec((OUT_TM, d), lambda i: (i, 0)),
                  pl.BlockSpec((OUT_TM, ka), lambda i: (i, 0)),
                  pl.BlockSpec((OUT_TM, kb), lambda i: (i, 0)),
                  pl.BlockSpec((ka, d), lambda i: (0, 0), **resident),
                  pl.BlockSpec((kb, d), lambda i: (0, 0), **resident)],
        out_specs=pl.BlockSpec((OUT_TM, d), lambda i: (i, 0)),
        out_shape=jax.ShapeDtypeStruct((t, d), F32),
        compiler_params=_params(("arbitrary",), est),
        name="out_proj",
    )(h, a, b, wa, wb)


def _ffn_body(*refs, tiles_per_seq, final_norm):
    if final_norm:
        h_ref, g_ref, wv_ref, wg_ref, cw_ref, cb_ref, wd_ref, gf_ref, o_ref, yn_ref, tail_ref = refs
    else:
        h_ref, g_ref, wv_ref, wg_ref, cw_ref, cb_ref, wd_ref, o_ref, yn_ref, tail_ref = refs
    i = pl.program_id(0)
    j = pl.program_id(1)
    tm = FFN_TM

    @pl.when(j == 0)
    def _():
        x = h_ref[...]
        yn_ref[...] = (_rms_scale(x) * g_ref[...]).astype(BF16)
        o_ref[...] = x

    yn = yn_ref[...]
    first = i % tiles_per_seq == 0
    acts = []
    for c in range(FFN_TF // FFN_SUB):
        cols = slice(c * FFN_SUB, (c + 1) * FFN_SUB)
        val = _dot(yn, wv_ref[:, cols])
        gate = _dot(yn, wg_ref[:, cols])
        tail = jnp.where(first, 0.0, tail_ref[j, :, cols])
        tail_ref[j, :, cols] = gate[tm - HALO:, :]
        cv = _causal_conv(gate, tail, cw_ref[:, cols], cb_ref[:, cols], FFN_CONV)
        acts.append((_silu(cv) * val).astype(BF16))
    o_ref[...] += _dot(jnp.concatenate(acts, axis=1), wd_ref[...])

    if final_norm:
        @pl.when(j == pl.num_programs(1) - 1)
        def _():
            o_ref[...] = _rms_scale(o_ref[...]) * gf_ref[...]


def _conv_ffn(h, g, w_up, conv_w, conv_b, w_down, seq, g_final=None):
    t, d = h.shape
    nj = D_FF // FFN_TF
    final_norm = g_final is not None
    in_specs = [pl.BlockSpec((FFN_TM, d), lambda i, j: (i, 0)),
                pl.BlockSpec((1, d), lambda i, j: (0, 0)),
                pl.BlockSpec((d, FFN_TF), lambda i, j: (0, j)),
                pl.BlockSpec((d, FFN_TF), lambda i, j: (0, nj + j)),
                pl.BlockSpec((FFN_CONV, FFN_TF), lambda i, j: (0, j)),
                pl.BlockSpec((1, FFN_TF), lambda i, j: (0, j)),
                pl.BlockSpec((FFN_TF, d), lambda i, j: (j, 0))]
    args = [h, g.reshape(1, d), w_up, w_up, conv_w, conv_b.reshape(1, D_FF), w_down]
    if final_norm:
        in_specs.append(pl.BlockSpec((1, d), lambda i, j: (0, 0)))
        args.append(g_final.reshape(1, d))
    est = (4 * FFN_TM * d * 4 + FFN_TM * d * 2 + 4 * d * FFN_TF * 2 + 2 * FFN_TF * d * 2
           + FFN_TM * FFN_TF * 2 + 6 * FFN_TM * FFN_SUB * 4)
    return pl.pallas_call(
        functools.partial(_ffn_body, tiles_per_seq=seq // FFN_TM, final_norm=final_norm),
        grid=(t // FFN_TM, nj),
        in_specs=in_specs,
        out_specs=pl.BlockSpec((FFN_TM, d), lambda i, j: (i, 0)),
        out_shape=jax.ShapeDtypeStruct((t, d), F32),
        scratch_shapes=[pltpu.VMEM((FFN_TM, d), BF16), pltpu.VMEM((nj, HALO, FFN_TF), F32)],
        compiler_params=_params(("arbitrary", "arbitrary"), est),
        name="conv_ffn_final" if final_norm else "conv_ffn",
    )(*args)


def _attn_ssd_layer(h, g, w_in, conv_w, conv_b, dt_bias, a_log, d_skip, ssd_norm, w_out, bsz, seq):
    w_main = w_in[:, :AB_MAIN].astype(BF16)
    w_dt = jnp.pad(w_in[:, AB_MAIN:], ((0, 0), (0, LANES - SSD_HEADS))).astype(BF16)
    proj, dt_raw = _norm_proj(h, g, w_main, w_dt)
    o_a = _sb_attention(proj, bsz, seq)
    o_b = _ssd_mixer(proj, dt_raw, conv_w, conv_b, dt_bias, a_log, d_skip, ssd_norm, bsz, seq)
    return _out_proj(h, o_a, o_b, w_out.astype(BF16))


def _lru_s5_layer(h, g, w_in, conv_w, conv_b, w_a, b_a, w_i, b_i, lam, log_dt, a_re, a_im, b_re, b_im,
                  c_re, c_im, d_skip, w_glu, b_glu, w_out, bsz, seq):
    proj = _norm_proj(h, g, w_in.astype(BF16))
    o_c = _lru_mixer(proj, conv_w, conv_b, w_a, b_a, w_i, b_i, lam, bsz, seq)
    o_d = _s5_mixer(proj, log_dt, a_re, a_im, b_re, b_im, c_re, c_im, d_skip, w_glu, b_glu, bsz, seq)
    return _out_proj(h, o_c, o_d, w_out.astype(BF16))


def kernel(x, norm_mix, norm_ffn, norm_final, ab_w_in, ab_conv_w, ab_conv_b, ab_dt_bias, ab_a_log, ab_d_skip, ab_ssd_norm, ab_w_out, cd_w_in, cd_conv_w, cd_conv_b, cd_w_a, cd_b_a, cd_w_i, cd_b_i, cd_lambda, cd_log_dt, cd_a_re, cd_a_im, cd_b_re, cd_b_im, cd_c_re, cd_c_im, cd_d_skip, cd_w_glu, cd_b_glu, cd_w_out, ffn_w_up, ffn_conv_w, ffn_conv_b, ffn_w_down):
    bsz, seq, d = x.shape
    depth = norm_mix.shape[0]
    h = x.reshape(bsz * seq, d)
    for layer in range(depth):
        j = layer // 2
        if layer % 2 == 0:
            h = _attn_ssd_layer(h, norm_mix[layer], ab_w_in[j], ab_conv_w[j], ab_conv_b[j], ab_dt_bias[j],
                                ab_a_log[j], ab_d_skip[j], ab_ssd_norm[j], ab_w_out[j], bsz, seq)
        else:
            h = _lru_s5_layer(h, norm_mix[layer], cd_w_in[j], cd_conv_w[j], cd_conv_b[j], cd_w_a[j], cd_b_a[j],
                              cd_w_i[j], cd_b_i[j], cd_lambda[j], cd_log_dt[j], cd_a_re[j], cd_a_im[j],
                              cd_b_re[j], cd_b_im[j], cd_c_re[j], cd_c_im[j], cd_d_skip[j], cd_w_glu[j],
                              cd_b_glu[j], cd_w_out[j], bsz, seq)
        h = _conv_ffn(h, norm_ffn[layer], ffn_w_up[layer].astype(BF16), ffn_conv_w[layer], ffn_conv_b[layer],
                      ffn_w_down[layer].astype(BF16), seq, g_final=norm_final if layer == depth - 1 else None)
    return h.reshape(bsz, seq, d)
```

```python
import functools

import jax
import jax.numpy as jnp
from jax import lax
from jax.experimental import pallas as pl
from jax.experimental.pallas import tpu as pltpu

F32 = jnp.float32
BF16 = jnp.bfloat16

D_MODEL = 2048
SB_HEADS = 4
SB_HEAD_DIM = 128
SB_WIDTH = SB_HEADS * SB_HEAD_DIM
SSD_HEADS = 24
SSD_HEAD_DIM = 64
SSD_WIDTH = SSD_HEADS * SSD_HEAD_DIM
SSD_GROUPS = 4
SSD_HEADS_PER_GROUP = SSD_HEADS // SSD_GROUPS
SSD_STATE = 128
SSD_CONV = 4
SSD_CHUNK = 128
SSD_BC = SSD_GROUPS * SSD_STATE
AB_MAIN = 3 * SB_WIDTH + 2 * SSD_WIDTH + 2 * SSD_BC
LRU_WIDTH = 1536
LRU_BLOCK = 128
LRU_BLOCKS = LRU_WIDTH // LRU_BLOCK
LRU_CONV = 4
LRU_C = 8.0
S5_GROUPS = 32
S5_GROUP_CH = 16
S5_WIDTH = S5_GROUPS * S5_GROUP_CH
S5_STATE = 64
S5_CHUNK = 128
S5_STATE_WIDTH = S5_GROUPS * S5_STATE
CD_IN = 2 * LRU_WIDTH + S5_WIDTH
D_FF = 4096
FFN_CONV = 3
EPS = 1e-6

LANES = 128
SUBLANES = 8
VMEM_CAP_BYTES = 60000 * 1024

PROJ_TM = 512
PROJ_TN = 512
OUT_TM = 512
OUT_TN = 512
FFN_TM = 512
FFN_TF = 1024
FFN_SUB = 256
SB_TQ = 256
SB_DEAD_LOG2 = 1e30
LRU_TT = 256
S5_TT = 256
SSD_TILE = 256
HALO = SUBLANES


def _vmem_limit(est_bytes):
    return int(min(max(est_bytes * 5 // 4, 32 * 1024 * 1024), VMEM_CAP_BYTES))


def _params(semantics, est_bytes):
    return pltpu.CompilerParams(dimension_semantics=semantics, vmem_limit_bytes=_vmem_limit(est_bytes))


def _dot(a, b):
    return jnp.dot(a, b, preferred_element_type=F32)


def _dot_nt(a, b):
    return lax.dot_general(a, b, (((1,), (1,)), ((), ())), preferred_element_type=F32)


def _dot_tn(a, b):
    return lax.dot_general(a, b, (((0,), (0,)), ((), ())), preferred_element_type=F32)


def _split_dot(x, m, terms, x_on_left=True):
    acc = None
    r = x
    for _ in range(terms):
        p = r.astype(BF16)
        d = _dot(p, m) if x_on_left else _dot(m, p)
        acc = d if acc is None else acc + d
        r = r - p.astype(F32)
    return acc


def _softplus(x):
    return jnp.maximum(x, 0.0) + jnp.log(1.0 + jnp.exp(-jnp.abs(x)))


def _sigmoid(x):
    return 0.5 + 0.5 * jnp.tanh(0.5 * x)


def _silu(x):
    hx = 0.5 * x
    return hx + hx * jnp.tanh(hx)


def _gelu_tanh(x):
    hx = 0.5 * x
    return hx + hx * jnp.tanh(x * (0.7978845608028654 + (0.7978845608028654 * 0.044715) * (x * x)))


def _rms_scale(x):
    return x * lax.rsqrt(jnp.mean(x * x, axis=-1, keepdims=True) + EPS)


def _shift_in_group(x, d, fill):
    rows, width = x.shape
    x3 = x.reshape(rows // SUBLANES, SUBLANES, width)
    sub = lax.broadcasted_iota(jnp.int32, x3.shape, 1)
    return jnp.where(sub >= d, pltpu.roll(x3, d, axis=1), fill).reshape(rows, width)


def _last_row_bcast(x):
    return jnp.broadcast_to(x[SUBLANES - 1:SUBLANES, :], x.shape)


def _rows_back(cur, tail, shift):
    rows, width = cur.shape
    groups = rows // SUBLANES
    rolled = pltpu.roll(cur.reshape(groups, SUBLANES, width), shift, axis=1)
    rolled_tail = pltpu.roll(tail.reshape(1, SUBLANES, width), shift, axis=1)
    prev = jnp.concatenate([rolled_tail, rolled[:groups - 1]], axis=0)
    sub = lax.broadcasted_iota(jnp.int32, rolled.shape, 1)
    return jnp.where(sub >= shift, rolled, prev).reshape(rows, width)


def _causal_conv(cur, tail, w, bias, taps):
    y = bias + w[taps - 1:taps, :] * cur
    for k in range(taps - 1):
        y = y + w[k:k + 1, :] * _rows_back(cur, tail, taps - 1 - k)
    return y


def _causal_conv_carried(cur, tail_ref, w, bias, taps):
    tail = tail_ref[...]
    tail_ref[...] = cur[cur.shape[0] - HALO:, :]
    return _causal_conv(cur, tail, w, bias, taps)


def _norm_proj_body(*refs, has_aux):
    if has_aux:
        h_ref, g_ref, w_ref, wa_ref, o_ref, oa_ref = refs
    else:
        h_ref, g_ref, w_ref, o_ref = refs
    yn = (_rms_scale(h_ref[...]) * g_ref[...]).astype(BF16)
    for c in range(w_ref.shape[1] // PROJ_TN):
        cols = slice(c * PROJ_TN, (c + 1) * PROJ_TN)
        o_ref[:, cols] = _dot(yn, w_ref[:, cols]).astype(o_ref.dtype)
    if has_aux:
        oa_ref[...] = _dot(yn, wa_ref[...])


def _norm_proj(h, g, w, w_aux=None):
    t, d = h.shape
    n = w.shape[1]
    has_aux = w_aux is not None
    resident = dict(pipeline_mode=pl.Buffered(1))
    in_specs = [pl.BlockSpec((PROJ_TM, d), lambda i: (i, 0)),
                pl.BlockSpec((1, d), lambda i: (0, 0)),
                pl.BlockSpec((d, n), lambda i: (0, 0), **resident)]
    out_shape = [jax.ShapeDtypeStruct((t, n), BF16)]
    out_specs = [pl.BlockSpec((PROJ_TM, n), lambda i: (i, 0))]
    args = [h, g.reshape(1, d), w]
    if has_aux:
        in_specs.append(pl.BlockSpec((d, LANES), lambda i: (0, 0), **resident))
        out_shape.append(jax.ShapeDtypeStruct((t, LANES), F32))
        out_specs.append(pl.BlockSpec((PROJ_TM, LANES), lambda i: (i, 0)))
        args.append(w_aux)
    est = (d * (n + LANES) * 2 + 2 * PROJ_TM * d * 4 + 2 * PROJ_TM * (n * 2 + LANES * 4)
           + PROJ_TM * d * 2 + 2 * PROJ_TM * PROJ_TN * 4)
    out = pl.pallas_call(
        functools.partial(_norm_proj_body, has_aux=has_aux),
        grid=(t // PROJ_TM,),
        in_specs=in_specs, out_specs=out_specs, out_shape=out_shape,
        compiler_params=_params(("arbitrary",), est),
        name="norm_proj_aux" if has_aux else "norm_proj",
    )(*args)
    return out if has_aux else out[0]


def _sb_body(q_ref, k_ref, v_ref, o_ref):
    i = pl.program_id(1)
    tq = SB_TQ
    dh = SB_HEAD_DIM
    row = lax.broadcasted_iota(jnp.int32, (tq, tq), 0)
    col = lax.broadcasted_iota(jnp.int32, (tq, tq), 1)
    tri = (row >= col).astype(BF16)
    strict = col < row
    qscale = (dh ** -0.5) * 1.4426950408889634
    qs = [(q_ref[:, h * dh:(h + 1) * dh].astype(F32) * qscale).astype(BF16) for h in range(SB_HEADS)]

    def block(h, start, masked, nls, acc):
        k = k_ref[pl.ds(start, tq), h * dh:(h + 1) * dh]
        v = v_ref[pl.ds(start, tq), h * dh:(h + 1) * dh]
        zb = _dot_nt(qs[h], k)
        nl = jnp.maximum(zb, 0.0) + jnp.log2(1.0 + jnp.exp2(-jnp.abs(zb)))
        if masked:
            nl = jnp.where(strict, nl, 0.0)
        suffix = _dot(nl.astype(BF16), tri)
        w = jnp.exp2(zb - suffix - nls)
        if masked:
            w = jnp.where(strict, w, 0.0)
        acc = acc + _dot(w.astype(BF16), v)
        nls = nls + jnp.sum(nl, axis=-1, keepdims=True)
        return nls, acc

    def step(kb, masked, carry):
        start = pl.multiple_of(kb * tq, tq)
        return tuple(block(h, start, masked, *carry[h]) for h in range(SB_HEADS))

    def live(carry):
        least = functools.reduce(jnp.minimum, [c[0] for c in carry])
        return jnp.min(least) < SB_DEAD_LOG2

    def body(state):
        n, _, carry = state
        carry = step(i - 1 - n, False, carry)
        return n + 1, live(carry), carry

    init = tuple((jnp.zeros((tq, 1), F32), jnp.zeros((tq, dh), F32)) for _ in range(SB_HEADS))
    carry = step(i, True, init)
    _, _, carry = lax.while_loop(lambda s: jnp.logical_and(s[0] < i, s[1]), body,
                                 (jnp.int32(0), live(carry), carry))
    o_ref[...] = jnp.concatenate([c[1] for c in carry], axis=1).astype(o_ref.dtype)


def _sb_attention(proj, bsz, seq):
    nq = seq // SB_TQ
    est = 2 * seq * SB_WIDTH * 2 + 64 * SB_TQ * SB_TQ * 4
    resident = dict(pipeline_mode=pl.Buffered(1))
    return pl.pallas_call(
        _sb_body,
        grid=(bsz, nq),
        in_specs=[pl.BlockSpec((SB_TQ, SB_WIDTH), lambda b, i: (b * nq + i, 0)),
                  pl.BlockSpec((seq, SB_WIDTH), lambda b, i: (b, 1), **resident),
                  pl.BlockSpec((seq, SB_WIDTH), lambda b, i: (b, 2), **resident)],
        out_specs=pl.BlockSpec((SB_TQ, SB_WIDTH), lambda b, i: (b * nq + i, 0)),
        out_shape=jax.ShapeDtypeStruct((bsz * seq, SB_WIDTH), BF16),
        compiler_params=_params(("arbitrary", "arbitrary"), est),
        name="sb_attention",
    )(proj, proj, proj)


def _ssd_body(z_ref, x_ref, b_ref, c_ref, dt_ref, cwx_ref, cwb_ref, cwc_ref, cbx_ref, cbb_ref, cbc_ref,
              dtb_ref, alog_ref, dsk_ref, ng_ref, e_ref, o_ref, tx_ref, tb_ref, tc_ref, st_ref):
    L = SSD_CHUNK

    @pl.when(pl.program_id(1) == 0)
    def _():
        tx_ref[...] = jnp.zeros_like(tx_ref)
        tb_ref[...] = jnp.zeros_like(tb_ref)
        tc_ref[...] = jnp.zeros_like(tc_ref)
        st_ref[...] = jnp.zeros_like(st_ref)

    xs_t = _silu(_causal_conv_carried(x_ref[...].astype(F32), tx_ref, cwx_ref[...], cbx_ref[...], SSD_CONV))
    bm_t = _silu(_causal_conv_carried(b_ref[...].astype(F32), tb_ref, cwb_ref[...], cbb_ref[...], SSD_CONV))
    cm_t = _silu(_causal_conv_carried(c_ref[...].astype(F32), tc_ref, cwc_ref[...], cbc_ref[...], SSD_CONV))
    dt_t = _softplus(dt_ref[...] + dtb_ref[...])
    da_t = dt_t * (-jnp.exp(alog_ref[...]))
    row = lax.broadcasted_iota(jnp.int32, (L, L), 0)
    col = lax.broadcasted_iota(jnp.int32, (L, L), 1)
    causal = row >= col
    tri = causal.astype(BF16)
    expand = e_ref[...]
    low_half = lax.broadcasted_iota(jnp.int32, (L, LANES), 1) < SSD_HEAD_DIM
    state = st_ref[...]
    ys = []
    for ch in range(SSD_TILE // L):
        rows = slice(ch * L, (ch + 1) * L)
        state, y_ch = _ssd_chunk(xs_t[rows, :], bm_t[rows, :], cm_t[rows, :], dt_t[rows, :], da_t[rows, :],
                                 state, causal, tri, expand, low_half, dsk_ref[...])
        ys.append(y_ch)
    st_ref[...] = state
    y = jnp.concatenate(ys, axis=0) * _silu(z_ref[...].astype(F32))
    o_ref[...] = (_rms_scale(y) * ng_ref[...]).astype(o_ref.dtype)


def _ssd_chunk(xs, bm, cm, dt, da, prev, causal, tri, expand, low_half, dsk):
    L = SSD_CHUNK
    N = SSD_STATE
    GW = SSD_HEADS_PER_GROUP * SSD_HEAD_DIM
    cs = _split_dot(da, tri, 3, x_on_left=False)
    cs_t = cs.T
    cs_last = cs[L - 1:L, :]
    dt_e = _split_dot(dt, expand, 2)
    ecs_e = _split_dot(jnp.exp(cs), expand, 2)
    toend_e = _split_dot(jnp.exp(cs_last - cs), expand, 2)

    xdt = xs * dt_e
    bm_b = bm.astype(BF16)
    cm_b = cm.astype(BF16)
    xte_b = (xdt * toend_e).astype(BF16)
    prev_b = prev.astype(BF16)

    y_diag, y_off, new_state = [], [], []
    for g in range(SSD_GROUPS):
        cg = cm_b[:, g * N:(g + 1) * N]
        bg = bm_b[:, g * N:(g + 1) * N]
        cb = _dot_nt(cg, bg)
        for pair in range(SSD_HEADS_PER_GROUP // 2):
            h0 = g * SSD_HEADS_PER_GROUP + 2 * pair
            mats = []
            for h in (h0, h0 + 1):
                diff = cs[:, h:h + 1] - cs_t[h:h + 1, :]
                decay = jnp.exp(jnp.where(causal, diff, -jnp.inf))
                mats.append((cb * decay).astype(BF16))
            xp = xdt[:, h0 * SSD_HEAD_DIM:(h0 + 2) * SSD_HEAD_DIM]
            rhs = jnp.concatenate([jnp.where(low_half, xp, 0.0), jnp.where(low_half, 0.0, xp)], axis=0)
            y_diag.append(_dot(jnp.concatenate(mats, axis=1), rhs.astype(BF16)))
        y_off.append(_dot(cg, prev_b[:, g * GW:(g + 1) * GW]))
        new_state.append(_dot_tn(bg, xte_b[:, g * GW:(g + 1) * GW]))

    state = prev * ecs_e[L - 1:L, :] + jnp.concatenate(new_state, axis=1)
    return state, jnp.concatenate(y_diag, axis=1) + jnp.concatenate(y_off, axis=1) * ecs_e + xs * dsk


def _ssd_mixer(proj, dt_raw, conv_w, conv_b, dt_bias, a_log, d_skip, norm_g, bsz, seq):
    L = SSD_TILE
    nc = seq // L
    W = SSD_WIDTH
    pad = LANES - SSD_HEADS
    cwx, cwb, cwc = conv_w[:, :W], conv_w[:, W:W + SSD_BC], conv_w[:, W + SSD_BC:]
    cb2 = conv_b.reshape(1, -1)
    cbx, cbb, cbc = cb2[:, :W], cb2[:, W:W + SSD_BC], cb2[:, W + SSD_BC:]
    dtb = jnp.pad(dt_bias, (0, pad)).reshape(1, LANES)
    alog = jnp.pad(a_log, (0, pad)).reshape(1, LANES)
    dsk = jnp.repeat(d_skip, SSD_HEAD_DIM).reshape(1, W)
    head_of_channel = jnp.arange(W, dtype=jnp.int32) // SSD_HEAD_DIM
    expand = (jnp.arange(LANES, dtype=jnp.int32)[:, None] == head_of_channel[None, :]).astype(BF16)

    def rows(b, c):
        return b * nc + c

    def const(shape):
        return pl.BlockSpec(shape, lambda b, c: (0, 0))

    zcol = 3 * SB_WIDTH // W
    bcol = (3 * SB_WIDTH + 2 * W) // SSD_BC
    est = 2 * (L * AB_MAIN * 2 + L * LANES * 4) + 2 * LANES * W * 2 + N_STATE_BYTES + 40 * L * W * 4
    return pl.pallas_call(
        _ssd_body,
        grid=(bsz, nc),
        in_specs=[pl.BlockSpec((L, W), lambda b, c: (rows(b, c), zcol)),
                  pl.BlockSpec((L, W), lambda b, c: (rows(b, c), zcol + 1)),
                  pl.BlockSpec((L, SSD_BC), lambda b, c: (rows(b, c), bcol)),
                  pl.BlockSpec((L, SSD_BC), lambda b, c: (rows(b, c), bcol + 1)),
                  pl.BlockSpec((L, LANES), lambda b, c: (rows(b, c), 0)),
                  const((SSD_CONV, W)), const((SSD_CONV, SSD_BC)), const((SSD_CONV, SSD_BC)),
                  const((1, W)), const((1, SSD_BC)), const((1, SSD_BC)),
                  const((1, LANES)), const((1, LANES)), const((1, W)), const((1, W)),
                  const((LANES, W))],
        out_specs=pl.BlockSpec((L, W), lambda b, c: (rows(b, c), 0)),
        out_shape=jax.ShapeDtypeStruct((bsz * seq, W), BF16),
        scratch_shapes=[pltpu.VMEM((HALO, W), F32), pltpu.VMEM((HALO, SSD_BC), F32),
                        pltpu.VMEM((HALO, SSD_BC), F32), pltpu.VMEM((SSD_STATE, W), F32)],
        compiler_params=_params(("arbitrary", "arbitrary"), est),
        name="ssd_mixer",
    )(proj, proj, proj, proj, dt_raw, cwx, cwb, cwc, cbx, cbb, cbc, dtb, alog, dsk,
      norm_g.reshape(1, W), expand)


N_STATE_BYTES = SSD_STATE * SSD_WIDTH * 4


def _lru_body(g_ref, x_ref, cw_ref, cb_ref, wa_ref, ba_ref, wi_ref, bi_ref, lam_ref, o_ref, tail_ref, h_ref):
    tt = LRU_TT

    @pl.when(pl.program_id(1) == 0)
    def _():
        tail_ref[...] = jnp.zeros_like(tail_ref)
        h_ref[...] = jnp.zeros_like(h_ref)

    xc = _causal_conv_carried(x_ref[...].astype(F32), tail_ref, cw_ref[...], cb_ref[...], LRU_CONV)
    xc_b = xc.astype(BF16)
    pre_a, pre_i = [], []
    for blk in range(LRU_BLOCKS):
        xb = xc_b[:, blk * LRU_BLOCK:(blk + 1) * LRU_BLOCK]
        pre_a.append(_dot(xb, wa_ref[blk]))
        pre_i.append(_dot(xb, wi_ref[blk]))
    r = _sigmoid(jnp.concatenate(pre_a, axis=1) + ba_ref[...])
    gate_i = _sigmoid(jnp.concatenate(pre_i, axis=1) + bi_ref[...])
    log_a = (-LRU_C * r) * _softplus(-lam_ref[...])
    a = jnp.exp(log_a)
    m = -jnp.tanh(log_a) * (a * a + 1.0)
    u = jnp.where(m > 0.0, m * lax.rsqrt(m), 0.0) * (gate_i * xc)

    d = 1
    while d < SUBLANES:
        u = u + a * _shift_in_group(u, d, 0.0)
        a = a * _shift_in_group(a, d, 1.0)
        d *= 2
    carry = h_ref[...]
    groups = []
    for g in range(tt // SUBLANES):
        rows = slice(g * SUBLANES, (g + 1) * SUBLANES)
        groups.append(u[rows, :] + a[rows, :] * carry)
        carry = _last_row_bcast(groups[-1])
    h_ref[...] = carry
    h = jnp.concatenate(groups, axis=0)
    o_ref[...] = (h * _gelu_tanh(g_ref[...].astype(F32))).astype(o_ref.dtype)


def _lru_mixer(proj, conv_w, conv_b, w_a, b_a, w_i, b_i, lam, bsz, seq):
    W = LRU_WIDTH
    nt = seq // LRU_TT

    def const2(shape):
        return pl.BlockSpec(shape, lambda b, t: (0, 0))

    def const3(shape):
        return pl.BlockSpec(shape, lambda b, t: (0, 0, 0))

    est = 4 * LRU_TT * W * 2 + 4 * LRU_BLOCKS * LRU_BLOCK * LRU_BLOCK * 2 + 30 * LRU_TT * W * 4
    return pl.pallas_call(
        _lru_body,
        grid=(bsz, nt),
        in_specs=[pl.BlockSpec((LRU_TT, W), lambda b, t: (b * nt + t, 0)),
                  pl.BlockSpec((LRU_TT, W), lambda b, t: (b * nt + t, 1)),
                  const2((LRU_CONV, W)), const2((1, W)),
                  const3((LRU_BLOCKS, LRU_BLOCK, LRU_BLOCK)), const2((1, W)),
                  const3((LRU_BLOCKS, LRU_BLOCK, LRU_BLOCK)), const2((1, W)),
                  const2((1, W))],
        out_specs=pl.BlockSpec((LRU_TT, W), lambda b, t: (b * nt + t, 0)),
        out_shape=jax.ShapeDtypeStruct((bsz * seq, W), BF16),
        scratch_shapes=[pltpu.VMEM((HALO, W), F32), pltpu.VMEM((SUBLANES, W), F32)],
        compiler_params=_params(("arbitrary", "arbitrary"), est),
        name="rglru_mixer",
    )(proj, proj, conv_w, conv_b.reshape(1, W), w_a.astype(BF16), b_a.reshape(1, W),
      w_i.astype(BF16), b_i.reshape(1, W), lam.reshape(1, W))


def _s5_body(u_ref, wb_ref, wc_ref, pwr_ref, pwi_ref, dsk_ref, wg_ref, bg_ref, o_ref, hr_ref, hi_ref):
    tt = S5_TT
    SW = S5_STATE_WIDTH

    @pl.when(pl.program_id(1) == 0)
    def _():
        hr_ref[...] = jnp.zeros_like(hr_ref)
        hi_ref[...] = jnp.zeros_like(hi_ref)

    u_b = u_ref[...]
    bu = _dot(u_b, wb_ref[...])
    sr, si = bu[:, :SW], bu[:, SW:]
    pwr = pwr_ref[...]
    pwi = pwi_ref[...]
    d = 1
    while d < SUBLANES:
        ar, ai = pwr[d - 1:d, :], pwi[d - 1:d, :]
        sr_s, si_s = _shift_in_group(sr, d, 0.0), _shift_in_group(si, d, 0.0)
        sr, si = sr + (ar * sr_s - ai * si_s), si + (ar * si_s + ai * sr_s)
        d *= 2
    cr, ci = hr_ref[...], hi_ref[...]
    groups_r, groups_i = [], []
    for g in range(tt // SUBLANES):
        rows = slice(g * SUBLANES, (g + 1) * SUBLANES)
        groups_r.append(sr[rows, :] + (pwr * cr - pwi * ci))
        groups_i.append(si[rows, :] + (pwr * ci + pwi * cr))
        cr, ci = _last_row_bcast(groups_r[-1]), _last_row_bcast(groups_i[-1])
    hr_ref[...] = cr
    hi_ref[...] = ci
    sr = jnp.concatenate(groups_r, axis=0)
    si = jnp.concatenate(groups_i, axis=0)
    s_cat = jnp.concatenate([sr, si], axis=1).astype(BF16)
    y = _gelu_tanh(_dot(s_cat, wc_ref[...]) + dsk_ref[...] * u_b.astype(F32))
    o_ref[...] = (y * _sigmoid(_dot(y.astype(BF16), wg_ref[...]) + bg_ref[...])).astype(o_ref.dtype)


def _s5_tables(log_dt, a_re, a_im, b_re, b_im, c_re, c_im):
    G, GC, N = S5_GROUPS, S5_GROUP_CH, S5_STATE
    dt = jnp.exp(log_dt)[:, None]
    lr = jnp.minimum(a_re, -1e-4)
    li = a_im
    mag = jnp.exp(dt * lr)
    ab_i = mag * jnp.sin(dt * li)
    am1_r = jnp.expm1(dt * lr) * jnp.cos(dt * li) - 2.0 * jnp.square(jnp.sin(0.5 * dt * li))
    den = lr * lr + li * li
    f_r = (am1_r * lr + ab_i * li) / den
    f_i = (ab_i * lr - am1_r * li) / den
    bb_r = f_r[..., None] * b_re - f_i[..., None] * b_im
    bb_i = f_r[..., None] * b_im + f_i[..., None] * b_re
    eye = jnp.eye(G, dtype=F32)
    wb_r = jnp.einsum('gnc,gh->gchn', bb_r, eye).reshape(G * GC, G * N)
    wb_i = jnp.einsum('gnc,gh->gchn', bb_i, eye).reshape(G * GC, G * N)
    wc_r = jnp.einsum('gcn,gh->gnhc', c_re, eye).reshape(G * N, G * GC)
    wc_i = jnp.einsum('gcn,gh->gnhc', c_im, eye).reshape(G * N, G * GC)
    wb = jnp.concatenate([wb_r, wb_i], axis=1).astype(BF16)
    wc = jnp.concatenate([wc_r, -wc_i], axis=0).astype(BF16)
    powers = jnp.arange(1, SUBLANES + 1, dtype=F32)[:, None, None] * dt[None]
    pw_mag = jnp.exp(powers * lr)
    pw_r = (pw_mag * jnp.cos(powers * li)).reshape(SUBLANES, G * N)
    pw_i = (pw_mag * jnp.sin(powers * li)).reshape(SUBLANES, G * N)
    return wb, wc, pw_r, pw_i


def _s5_mixer(proj, log_dt, a_re, a_im, b_re, b_im, c_re, c_im, d_skip, w_glu, b_glu, bsz, seq):
    W = S5_WIDTH
    SW = S5_STATE_WIDTH
    tt = S5_TT
    nt = seq // tt
    wb, wc, pw_r, pw_i = _s5_tables(log_dt, a_re, a_im, b_re, b_im, c_re, c_im)

    def const(shape):
        return pl.BlockSpec(shape, lambda b, t: (0, 0))

    ucol = 2 * LRU_WIDTH // W
    est = 8 * W * SW * 2 + 4 * tt * SW * 4 + 2 * W * W * 2 + 24 * tt * SW * 4
    return pl.pallas_call(
        _s5_body,
        grid=(bsz, nt),
        in_specs=[pl.BlockSpec((tt, W), lambda b, t: (b * nt + t, ucol)),
                  const((W, 2 * SW)), const((2 * SW, W)), const((SUBLANES, SW)), const((SUBLANES, SW)),
                  const((1, W)), const((W, W)), const((1, W))],
        out_specs=pl.BlockSpec((tt, W), lambda b, t: (b * nt + t, 0)),
        out_shape=jax.ShapeDtypeStruct((bsz * seq, W), BF16),
        scratch_shapes=[pltpu.VMEM((SUBLANES, SW), F32), pltpu.VMEM((SUBLANES, SW), F32)],
        compiler_params=_params(("arbitrary", "arbitrary"), est),
        name="s5_mixer",
    )(proj, wb, wc, pw_r, pw_i, d_skip.reshape(1, W), w_glu.astype(BF16), b_glu.reshape(1, W))


def _out_proj_body(h_ref, a_ref, b_ref, wa_ref, wb_ref, o_ref):
    a = a_ref[...]
    b = b_ref[...]
    for c in range(o_ref.shape[1] // OUT_TN):
        cols = slice(c * OUT_TN, (c + 1) * OUT_TN)
        o_ref[:, cols] = h_ref[:, cols] + _dot(a, wa_ref[:, cols]) + _dot(b, wb_ref[:, cols])


def _out_proj(h, a, b, w):
    t, d = h.shape
    ka, kb = a.shape[1], b.shape[1]
    wa, wb = w[:ka], w[ka:]
    resident = dict(pipeline_mode=pl.Buffered(1))
    est = (ka + kb) * d * 2 + 4 * OUT_TM * d * 4 + 2 * OUT_TM * (ka + kb) * 2 + 2 * OUT_TM * OUT_TN * 4
    return pl.pallas_call(
        _out_proj_body,
        grid=(t // OUT_TM,),
        in_specs=[pl.BlockSpec((OUT_TM, d), lambda i: (i, 0)),
                  pl.BlockSpec((OUT_TM, ka), lambda i: (i, 0)),
                  pl.BlockSpec((OUT_TM, kb), lambda i: (i, 0)),
                  pl.BlockSpec((ka, d), lambda i: (0, 0), **resident),
                  pl.BlockSpec((kb, d), lambda i: (0, 0), **resident)],
        out_specs=pl.BlockSpec((OUT_TM, d), lambda i: (i, 0)),
        out_shape=jax.ShapeDtypeStruct((t, d), F32),
        compiler_params=_params(("arbitrary",), est),
        name="out_proj",
    )(h, a, b, wa, wb)


def _ffn_body(*refs, tiles_per_seq, final_norm):
    if final_norm:
        h_ref, g_ref, wv_ref, wg_ref, cw_ref, cb_ref, wd_ref, gf_ref, o_ref, yn_ref, tail_ref = refs
    else:
        h_ref, g_ref, wv_ref, wg_ref, cw_ref, cb_ref, wd_ref, o_ref, yn_ref, tail_ref = refs
    i = pl.program_id(0)
    j = pl.program_id(1)
    tm = FFN_TM

    @pl.when(j == 0)
    def _():
        x = h_ref[...]
        yn_ref[...] = (_rms_scale(x) * g_ref[...]).astype(BF16)
        o_ref[...] = x

    yn = yn_ref[...]
    first = i % tiles_per_seq == 0
    acts = []
    for c in range(FFN_TF // FFN_SUB):
        cols = slice(c * FFN_SUB, (c + 1) * FFN_SUB)
        val = _dot(yn, wv_ref[:, cols])
        gate = _dot(yn, wg_ref[:, cols])
        tail = jnp.where(first, 0.0, tail_ref[j, :, cols])
        tail_ref[j, :, cols] = gate[tm - HALO:, :]
        cv = _causal_conv(gate, tail, cw_ref[:, cols], cb_ref[:, cols], FFN_CONV)
        acts.append((_silu(cv) * val).astype(BF16))
    o_ref[...] += _dot(jnp.concatenate(acts, axis=1), wd_ref[...])

    if final_norm:
        @pl.when(j == pl.num_programs(1) - 1)
        def _():
            o_ref[...] = _rms_scale(o_ref[...]) * gf_ref[...]


def _conv_ffn(h, g, w_up, conv_w, conv_b, w_down, seq, g_final=None):
    t, d = h.shape
    nj = D_FF // FFN_TF
    final_norm = g_final is not None
    in_specs = [pl.BlockSpec((FFN_TM, d), lambda i, j: (i, 0)),
                pl.BlockSpec((1, d), lambda i, j: (0, 0)),
                pl.BlockSpec((d, FFN_TF), lambda i, j: (0, j)),
                pl.BlockSpec((d, FFN_TF), lambda i, j: (0, nj + j)),
                pl.BlockSpec((FFN_CONV, FFN_TF), lambda i, j: (0, j)),
                pl.BlockSpec((1, FFN_TF), lambda i, j: (0, j)),
                pl.BlockSpec((FFN_TF, d), lambda i, j: (j, 0))]
    args = [h, g.reshape(1, d), w_up, w_up, conv_w, conv_b.reshape(1, D_FF), w_down]
    if final_norm:
        in_specs.append(pl.BlockSpec((1, d), lambda i, j: (0, 0)))
        args.append(g_final.reshape(1, d))
    est = (4 * FFN_TM * d * 4 + FFN_TM * d * 2 + 4 * d * FFN_TF * 2 + 2 * FFN_TF * d * 2
           + FFN_TM * FFN_TF * 2 + 6 * FFN_TM * FFN_SUB * 4)
    return pl.pallas_call(
        functools.partial(_ffn_body, tiles_per_seq=seq // FFN_TM, final_norm=final_norm),
        grid=(t // FFN_TM, nj),
        in_specs=in_specs,
        out_specs=pl.BlockSpec((FFN_TM, d), lambda i, j: (i, 0)),
        out_shape=jax.ShapeDtypeStruct((t, d), F32),
        scratch_shapes=[pltpu.VMEM((FFN_TM, d), BF16), pltpu.VMEM((nj, HALO, FFN_TF), F32)],
        compiler_params=_params(("arbitrary", "arbitrary"), est),
        name="conv_ffn_final" if final_norm else "conv_ffn",
    )(*args)


def _attn_ssd_layer(h, g, w_in, conv_w, conv_b, dt_bias, a_log, d_skip, ssd_norm, w_out, bsz, seq):
    w_main = w_in[:, :AB_MAIN].astype(BF16)
    w_dt = jnp.pad(w_in[:, AB_MAIN:], ((0, 0), (0, LANES - SSD_HEADS))).astype(BF16)
    proj, dt_raw = _norm_proj(h, g, w_main, w_dt)
    o_a = _sb_attention(proj, bsz, seq)
    o_b = _ssd_mixer(proj, dt_raw, conv_w, conv_b, dt_bias, a_log, d_skip, ssd_norm, bsz, seq)
    return _out_proj(h, o_a, o_b, w_out.astype(BF16))


def _lru_s5_layer(h, g, w_in, conv_w, conv_b, w_a, b_a, w_i, b_i, lam, log_dt, a_re, a_im, b_re, b_im,
                  c_re, c_im, d_skip, w_glu, b_glu, w_out, bsz, seq):
    proj = _norm_proj(h, g, w_in.astype(BF16))
    o_c = _lru_mixer(proj, conv_w, conv_b, w_a, b_a, w_i, b_i, lam, bsz, seq)
    o_d = _s5_mixer(proj, log_dt, a_re, a_im, b_re, b_im, c_re, c_im, d_skip, w_glu, b_glu, bsz, seq)
    return _out_proj(h, o_c, o_d, w_out.astype(BF16))


def kernel(x, norm_mix, norm_ffn, norm_final, ab_w_in, ab_conv_w, ab_conv_b, ab_dt_bias, ab_a_log, ab_d_skip, ab_ssd_norm, ab_w_out, cd_w_in, cd_conv_w, cd_conv_b, cd_w_a, cd_b_a, cd_w_i, cd_b_i, cd_lambda, cd_log_dt, cd_a_re, cd_a_im, cd_b_re, cd_b_im, cd_c_re, cd_c_im, cd_d_skip, cd_w_glu, cd_b_glu, cd_w_out, ffn_w_up, ffn_conv_w, ffn_conv_b, ffn_w_down):
    bsz, seq, d = x.shape
    depth = norm_mix.shape[0]
    h = x.reshape(bsz * seq, d)
    for layer in range(depth):
        j = layer // 2
        if layer % 2 == 0:
            h = _attn_ssd_layer(h, norm_mix[layer], ab_w_in[j], ab_conv_w[j], ab_conv_b[j], ab_dt_bias[j],
                                ab_a_log[j], ab_d_skip[j], ab_ssd_norm[j], ab_w_out[j], bsz, seq)
        else:
            h = _lru_s5_layer(h, norm_mix[layer], cd_w_in[j], cd_conv_w[j], cd_conv_b[j], cd_w_a[j], cd_b_a[j],
                              cd_w_i[j], cd_b_i[j], cd_lambda[j], cd_log_dt[j], cd_a_re[j], cd_a_im[j],
                              cd_b_re[j], cd_b_im[j], cd_c_re[j], cd_c_im[j], cd_d_skip[j], cd_w_glu[j],
                              cd_b_glu[j], cd_w_out[j], bsz, seq)
        h = _conv_ffn(h, norm_ffn[layer], ffn_w_up[layer].astype(BF16), ffn_conv_w[layer], ffn_conv_b[layer],
                      ffn_w_down[layer].astype(BF16), seq, g_final=norm_final if layer == depth - 1 else None)
    return h.reshape(bsz, seq, d)
```
